```python
import math
import jax, jax.numpy as jnp
from jax import lax
import numpy as np

D_MODEL = 2048
BATCH = 2
SEQ = 16384
DEPTH = 4
DEC_BATCH = 1
DEC_SEQ = 16384
PAST_LEN = 128

HEAD_DIM = 128
N_Q_HEADS = 8
N_KV_HEADS = 2
Q_PER_KV = N_Q_HEADS // N_KV_HEADS
ATTN_WIDTH = N_Q_HEADS * HEAD_DIM
CONV_WIDTH = D_MODEL - ATTN_WIDTH
KV_WIDTH = N_KV_HEADS * HEAD_DIM
IN_WIDTH = ATTN_WIDTH + 2 * KV_WIDTH + 2 * CONV_WIDTH
WINDOW = 128
BLOCK = 128
CONV_KERNEL = 31
CONV_PAD = CONV_KERNEL // 2
D_FF = 4 * D_MODEL
ROPE_THETA = 10000.0
NORM_EPS = 1e-5
MASK_VALUE = -1e30

kernel_name = "hybrid_swa_conformer_encoder"


def rms_norm(x, g):
    xf = x.astype(jnp.float32)
    y = xf * lax.rsqrt(jnp.mean(xf * xf, axis=-1, keepdims=True) + NORM_EPS) * g.astype(jnp.float32)
    return y.astype(x.dtype)


def layer_norm(x, g, b):
    xf = x.astype(jnp.float32)
    mu = jnp.mean(xf, axis=-1, keepdims=True)
    xc = xf - mu
    var = jnp.mean(xc * xc, axis=-1, keepdims=True)
    y = xc * lax.rsqrt(var + NORM_EPS) * g.astype(jnp.float32) + b.astype(jnp.float32)
    return y.astype(x.dtype)


def apply_rope(t):
    s = t.shape[1]
    half = HEAD_DIM // 2
    inv_freq = jnp.exp(-math.log(ROPE_THETA) * jnp.arange(half, dtype=jnp.float32) / half)
    ang = jnp.arange(s, dtype=jnp.float32)[:, None] * inv_freq[None, :]
    cos = jnp.cos(ang)[None, :, None, :]
    sin = jnp.sin(ang)[None, :, None, :]
    tf = t.astype(jnp.float32)
    t1, t2 = tf[..., :half], tf[..., half:]
    out = jnp.concatenate([t1 * cos - t2 * sin, t2 * cos + t1 * sin], axis=-1)
    return out.astype(t.dtype)


def banded_sink_attention(q, k, v, sink):
    b, s = q.shape[0], q.shape[1]
    nb = s // BLOCK
    qb = q.reshape(b, nb, BLOCK, N_KV_HEADS, Q_PER_KV, HEAD_DIM)

    def neighbours(t):
        tp = jnp.pad(t, ((0, 0), (BLOCK, BLOCK), (0, 0), (0, 0)))
        tp = tp.reshape(b, nb + 2, BLOCK, N_KV_HEADS, HEAD_DIM)
        return jnp.concatenate([tp[:, :-2], tp[:, 1:-1], tp[:, 2:]], axis=2)

    kn = neighbours(k)
    vn = neighbours(v)
    scores = jnp.einsum('bnqhgd,bnkhd->bnhgqk', qb, kn,
                        preferred_element_type=jnp.float32) * (HEAD_DIM ** -0.5)
    blk = jnp.arange(nb)[:, None, None]
    qpos = blk * BLOCK + jnp.arange(BLOCK)[None, :, None]
    kpos = (blk - 1) * BLOCK + jnp.arange(3 * BLOCK)[None, None, :]
    valid = (jnp.abs(kpos - qpos) <= WINDOW) & (kpos >= 0) & (kpos < s)
    valid = valid[None, :, None, None, :, :]
    scores = jnp.where(valid, scores, MASK_VALUE)
    sink_l = sink.astype(jnp.float32).reshape(1, 1, N_KV_HEADS, Q_PER_KV, 1, 1)
    m = jnp.maximum(jnp.max(scores, axis=-1, keepdims=True), sink_l)
    p = jnp.exp(scores - m)
    p = p / (jnp.sum(p, axis=-1, keepdims=True) + jnp.exp(sink_l - m))
    out = jnp.einsum('bnhgqk,bnkhd->bnqhgd', p.astype(v.dtype), vn)
    return out.reshape(b, s, ATTN_WIDTH)


def conformer_conv(a, gate, conv_w, conv_b, ln_g, ln_b):
    u = a * jax.nn.sigmoid(gate)
    y = lax.conv_general_dilated(
        u, conv_w[:, None, :].astype(u.dtype), window_strides=(1,),
        padding=[(CONV_PAD, CONV_PAD)], dimension_numbers=('NWC', 'WIO', 'NWC'),
        feature_group_count=CONV_WIDTH) + conv_b.astype(u.dtype)
    y = layer_norm(y, ln_g, ln_b)
    return jax.nn.silu(y)


def encoder_layer(x, norm_mix, w_in, sink, conv_w, conv_b, ln_g, ln_b, w_out, norm_ffn, w_up, w_down):
    b, s, _ = x.shape
    h = rms_norm(x, norm_mix)
    proj = h @ w_in
    o1 = ATTN_WIDTH
    o2 = o1 + KV_WIDTH
    o3 = o2 + KV_WIDTH
    o4 = o3 + CONV_WIDTH
    q = apply_rope(proj[..., :o1].reshape(b, s, N_Q_HEADS, HEAD_DIM))
    k = apply_rope(proj[..., o1:o2].reshape(b, s, N_KV_HEADS, HEAD_DIM))
    v = proj[..., o2:o3].reshape(b, s, N_KV_HEADS, HEAD_DIM)
    attn = banded_sink_attention(q, k, v, sink)
    conv = conformer_conv(proj[..., o3:o4], proj[..., o4:], conv_w, conv_b, ln_g, ln_b)
    x = x + jnp.concatenate([attn, conv], axis=-1) @ w_out
    h = rms_norm(x, norm_ffn)
    x = x + jnp.square(jax.nn.relu(h @ w_up)) @ w_down
    return x


def run_trunk(x, norm_mix, w_in, attn_sink, conv_w, conv_b, conv_ln_g, conv_ln_b,
              w_out, norm_ffn, w_up, w_down, final_norm):
    for i in range(DEPTH):
        x = encoder_layer(x, norm_mix[i], w_in[i], attn_sink[i], conv_w[i], conv_b[i],
                          conv_ln_g[i], conv_ln_b[i], w_out[i], norm_ffn[i], w_up[i], w_down[i])
    return rms_norm(x, final_norm)


def setup_inputs(seed: int = 0) -> dict:
    key = jax.random.key(seed)
    ks = jax.random.split(key, 16)
    f32 = jnp.float32
    x_prompt = jax.random.normal(ks[0], (BATCH, SEQ, D_MODEL), f32)
    x_sample = jax.random.normal(ks[1], (DEC_BATCH, DEC_SEQ, D_MODEL), f32)
    norm_mix = 1.0 + 0.02 * jax.random.normal(ks[2], (DEPTH, D_MODEL), f32)
    w_in = jax.random.normal(ks[3], (DEPTH, D_MODEL, IN_WIDTH), f32) * D_MODEL ** -0.5
    attn_sink = 0.5 * jax.random.normal(ks[4], (DEPTH, N_Q_HEADS), f32)
    conv_w = jax.random.normal(ks[5], (DEPTH, CONV_KERNEL, CONV_WIDTH), f32) * CONV_KERNEL ** -0.5
    conv_b = 0.02 * jax.random.normal(ks[6], (DEPTH, CONV_WIDTH), f32)
    conv_ln_g = 1.0 + 0.02 * jax.random.normal(ks[7], (DEPTH, CONV_WIDTH), f32)
    conv_ln_b = 0.02 * jax.random.normal(ks[8], (DEPTH, CONV_WIDTH), f32)
    w_out = jax.random.normal(ks[9], (DEPTH, D_MODEL, D_MODEL), f32) * D_MODEL ** -0.5
    norm_ffn = 1.0 + 0.02 * jax.random.normal(ks[10], (DEPTH, D_MODEL), f32)
    w_up = jax.random.normal(ks[11], (DEPTH, D_MODEL, D_FF), f32) * D_MODEL ** -0.5
    w_down = jax.random.normal(ks[12], (DEPTH, D_FF, D_MODEL), f32) * D_FF ** -0.5
    final_norm = 1.0 + 0.02 * jax.random.normal(ks[13], (D_MODEL,), f32)
    return {"x_prompt": x_prompt, "x_sample": x_sample, "norm_mix": norm_mix, "w_in": w_in,
            "attn_sink": attn_sink, "conv_w": conv_w, "conv_b": conv_b, "conv_ln_g": conv_ln_g,
            "conv_ln_b": conv_ln_b, "w_out": w_out, "norm_ffn": norm_ffn, "w_up": w_up,
            "w_down": w_down, "final_norm": final_norm}


def reference(x_prompt, x_sample, norm_mix, w_in, attn_sink, conv_w, conv_b, conv_ln_g,
              conv_ln_b, w_out, norm_ffn, w_up, w_down, final_norm):
    y_prompt = run_trunk(x_prompt, norm_mix, w_in, attn_sink, conv_w, conv_b, conv_ln_g,
                         conv_ln_b, w_out, norm_ffn, w_up, w_down, final_norm)
    y_sample = run_trunk(x_sample, norm_mix, w_in, attn_sink, conv_w, conv_b, conv_ln_g,
                         conv_ln_b, w_out, norm_ffn, w_up, w_down, final_norm)
    return (y_prompt, y_sample)
```

```python
import functools
import math

import jax
import jax.numpy as jnp
from jax import lax
from jax.experimental import pallas as pl
from jax.experimental.pallas import tpu as pltpu

D_MODEL = 2048
HEAD_DIM = 128
N_Q_HEADS = 8
N_KV_HEADS = 2
Q_PER_KV = N_Q_HEADS // N_KV_HEADS
ATTN_WIDTH = N_Q_HEADS * HEAD_DIM
CONV_WIDTH = D_MODEL - ATTN_WIDTH
KV_WIDTH = N_KV_HEADS * HEAD_DIM
IN_WIDTH = ATTN_WIDTH + 2 * KV_WIDTH + 2 * CONV_WIDTH
WINDOW = 128
BLOCK = 128
CONV_KERNEL = 31
CONV_PAD = CONV_KERNEL // 2
D_FF = 4 * D_MODEL
ROPE_THETA = 10000.0
NORM_EPS = 1e-5
MASK_VALUE = -1e30

V7X_VMEM_BYTES = 64 * 1024 * 1024
VMEM_LIMIT_BYTES = V7X_VMEM_BYTES - 8 * 1024 * 1024

TM_IN = 512
TM_MIX = 512
TM_MLP = 1024
TF_MLP = 512
HALO = 16
CONV_ROWS = 64
LANES = 128

BF16 = jnp.bfloat16
F32 = jnp.float32


def _rms_norm_f32(x, g):
    return x * lax.rsqrt(jnp.mean(x * x, axis=-1, keepdims=True) + NORM_EPS) * g


def _inproj_kernel(x_ref, g_ref, w_ref, cos_ref, sin_ref, q_ref, k_ref, v_ref, u_ref):
    h = _rms_norm_f32(x_ref[...], g_ref[...]).astype(BF16)
    cos = cos_ref[...]
    sin = sin_ref[...]

    def rope(t):
        return t * cos + pltpu.roll(t, HEAD_DIM // 2, 1) * sin

    o1 = ATTN_WIDTH
    o2 = o1 + KV_WIDTH
    o3 = o2 + KV_WIDTH
    o4 = o3 + CONV_WIDTH
    pq = jnp.dot(h, w_ref[:, 0:o1], preferred_element_type=F32)
    for hd in range(N_Q_HEADS):
        sl = slice(hd * HEAD_DIM, (hd + 1) * HEAD_DIM)
        q_ref[:, sl] = rope(pq[:, sl]).astype(BF16)
    pk = jnp.dot(h, w_ref[:, o1:o2], preferred_element_type=F32)
    for hd in range(N_KV_HEADS):
        sl = slice(hd * HEAD_DIM, (hd + 1) * HEAD_DIM)
        k_ref[:, sl] = rope(pk[:, sl]).astype(BF16)
    v_ref[...] = jnp.dot(h, w_ref[:, o2:o3], preferred_element_type=F32).astype(BF16)
    pa = jnp.dot(h, w_ref[:, o3:o4], preferred_element_type=F32)
    pg = jnp.dot(h, w_ref[:, o4:], preferred_element_type=F32)
    u_ref[...] = (pa * jax.nn.sigmoid(pg)).astype(BF16)


def _inproj(x, g, w_in, cos, sin, seq):
    t = x.shape[0]
    tm = TM_IN
    nseq_blk = seq // tm
    row = lambda i: (i, 0)
    const = lambda i: (0, 0)
    pos = lambda i: (i % nseq_blk, 0)
    return pl.pallas_call(
        _inproj_kernel,
        grid=(t // tm,),
        in_specs=[
            pl.BlockSpec((tm, D_MODEL), row),
            pl.BlockSpec((1, D_MODEL), const),
            pl.BlockSpec((D_MODEL, IN_WIDTH), const, pipeline_mode=pl.Buffered(1)),
            pl.BlockSpec((tm, HEAD_DIM), pos),
            pl.BlockSpec((tm, HEAD_DIM), pos),
        ],
        out_specs=[
            pl.BlockSpec((tm, ATTN_WIDTH), row),
            pl.BlockSpec((tm, KV_WIDTH), row),
            pl.BlockSpec((tm, KV_WIDTH), row),
            pl.BlockSpec((tm, CONV_WIDTH), row),
        ],
        out_shape=[
            jax.ShapeDtypeStruct((t, ATTN_WIDTH), BF16),
            jax.ShapeDtypeStruct((t, KV_WIDTH), BF16),
            jax.ShapeDtypeStruct((t, KV_WIDTH), BF16),
            jax.ShapeDtypeStruct((t, CONV_WIDTH), BF16),
        ],
        compiler_params=pltpu.CompilerParams(
            dimension_semantics=("parallel",), vmem_limit_bytes=VMEM_LIMIT_BYTES),
        name="inproj",
    )(x, g, w_in, cos, sin)


def _mixer_kernel(sink_ref, x_ref, q_ref, kp_ref, kc_ref, kn_ref, vp_ref, vc_ref, vn_ref,
                  up_ref, uc_ref, un_ref, cw_ref, cb_ref, lg_ref, lb_ref, wo_ref,
                  o_ref, kbuf, vbuf, ubuf, ybuf, mix, *, tiles_per_seq):
    tm = TM_MIX
    i = pl.program_id(0)
    first = (i % tiles_per_seq) == 0
    last = (i % tiles_per_seq) == tiles_per_seq - 1

    kbuf[0:BLOCK, :] = kp_ref[...]
    kbuf[BLOCK:BLOCK + tm, :] = kc_ref[...]
    kbuf[BLOCK + tm:, :] = kn_ref[...]
    vbuf[0:BLOCK, :] = vp_ref[...]
    vbuf[BLOCK:BLOCK + tm, :] = vc_ref[...]
    vbuf[BLOCK + tm:, :] = vn_ref[...]

    r = lax.broadcasted_iota(jnp.int32, (BLOCK, 3 * BLOCK), 0)
    c = lax.broadcasted_iota(jnp.int32, (BLOCK, 3 * BLOCK), 1)
    band = (c >= r) & (c <= r + 2 * WINDOW)
    lo = jnp.where(first, BLOCK, 0)
    hi = jnp.where(last, 2 * BLOCK, 3 * BLOCK)
    nblk = tm // BLOCK
    scale = HEAD_DIM ** -0.5
    for b in range(nblk):
        valid = band
        if b == 0:
            valid = valid & (c >= lo)
        if b == nblk - 1:
            valid = valid & (c < hi)
        rows = slice(b * BLOCK, (b + 1) * BLOCK)
        win = slice(b * BLOCK, (b + 3) * BLOCK)
        for kvh in range(N_KV_HEADS):
            kv_cols = slice(kvh * HEAD_DIM, (kvh + 1) * HEAD_DIM)
            kwin = kbuf[win, kv_cols]
            vwin = vbuf[win, kv_cols]
            qs = jnp.concatenate(
                [q_ref[rows, (kvh * Q_PER_KV + g) * HEAD_DIM:(kvh * Q_PER_KV + g + 1) * HEAD_DIM]
                 for g in range(Q_PER_KV)], axis=0)
            s = lax.dot_general(qs, kwin, (((1,), (1,)), ((), ())),
                                preferred_element_type=F32) * scale
            for g in range(Q_PER_KV):
                hd = kvh * Q_PER_KV + g
                sg = jnp.where(valid, s[g * BLOCK:(g + 1) * BLOCK, :], MASK_VALUE)
                sink = sink_ref[hd]
                m = jnp.maximum(jnp.max(sg, axis=-1, keepdims=True), sink)
                p = jnp.exp(sg - m)
                denom = jnp.sum(p, axis=-1, keepdims=True) + jnp.exp(sink - m)
                pv = jnp.dot(p.astype(BF16), vwin, preferred_element_type=F32)
                mix[rows, hd * HEAD_DIM:(hd + 1) * HEAD_DIM] = (pv / denom).astype(BF16)

    zero_halo = jnp.zeros((HALO, CONV_WIDTH), F32)
    ubuf[0:HALO, :] = jnp.where(first, zero_halo, up_ref[...].astype(F32))
    ubuf[HALO:HALO + tm, :] = uc_ref[...].astype(F32)
    ubuf[HALO + tm:, :] = jnp.where(last, zero_halo, un_ref[...].astype(F32))

    def conv_chunk(cc, carry):
        lanes = pl.ds(pl.multiple_of(cc * LANES, LANES), LANES)
        bias = cb_ref[:, lanes]
        for rc in range(tm // CONV_ROWS):
            acc = jnp.broadcast_to(bias, (CONV_ROWS, LANES))
            for j in range(CONV_KERNEL):
                start = HALO - CONV_PAD + j + rc * CONV_ROWS
                acc = acc + cw_ref[j:j + 1, lanes] * ubuf[start:start + CONV_ROWS, lanes]
            ybuf[rc * CONV_ROWS:(rc + 1) * CONV_ROWS, lanes] = acc
        return carry

    lax.fori_loop(0, CONV_WIDTH // LANES, conv_chunk, 0)

    y = ybuf[...]
    mu = jnp.mean(y, axis=-1, keepdims=True)
    yc = y - mu
    var = jnp.mean(yc * yc, axis=-1, keepdims=True)
    yn = yc * lax.rsqrt(var + NORM_EPS) * lg_ref[...] + lb_ref[...]
    mix[:, ATTN_WIDTH:] = (yn * jax.nn.sigmoid(yn)).astype(BF16)

    o_ref[...] = x_ref[...] + jnp.dot(mix[...], wo_ref[...], preferred_element_type=F32)


def _mixer(x, q, k, v, u, sink, cw, cb, lg, lb, w_out, seq):
    t = x.shape[0]
    tm = TM_MIX
    tiles_per_seq = seq // tm
    bpt = tm // BLOCK
    hpt = tm // HALO
    nkb = t // BLOCK
    nhb = t // HALO
    row = lambda i: (i, 0)
    const = lambda i: (0, 0)
    kprev = lambda i: (jnp.maximum(i * bpt - 1, 0), 0)
    knext = lambda i: (jnp.minimum((i + 1) * bpt, nkb - 1), 0)
    uprev = lambda i: (jnp.maximum(i * hpt - 1, 0), 0)
    unext = lambda i: (jnp.minimum((i + 1) * hpt, nhb - 1), 0)
    kern = functools.partial(_mixer_kernel, tiles_per_seq=tiles_per_seq)
    return pl.pallas_call(
        kern,
        grid=(t // tm,),
        in_specs=[
            pl.BlockSpec(memory_space=pltpu.SMEM),
            pl.BlockSpec((tm, D_MODEL), row),
            pl.BlockSpec((tm, ATTN_WIDTH), row),
            pl.BlockSpec((BLOCK, KV_WIDTH), kprev),
            pl.BlockSpec((tm, KV_WIDTH), row),
            pl.BlockSpec((BLOCK, KV_WIDTH), knext),
            pl.BlockSpec((BLOCK, KV_WIDTH), kprev),
            pl.BlockSpec((tm, KV_WIDTH), row),
            pl.BlockSpec((BLOCK, KV_WIDTH), knext),
            pl.BlockSpec((HALO, CONV_WIDTH), uprev),
            pl.BlockSpec((tm, CONV_WIDTH), row),
            pl.BlockSpec((HALO, CONV_WIDTH), unext),
            pl.BlockSpec((CONV_KERNEL + 1, CONV_WIDTH), const),
            pl.BlockSpec((1, CONV_WIDTH), const),
            pl.BlockSpec((1, CONV_WIDTH), const),
            pl.BlockSpec((1, CONV_WIDTH), const),
            pl.BlockSpec((D_MODEL, D_MODEL), const, pipeline_mode=pl.Buffered(1)),
        ],
        out_specs=pl.BlockSpec((tm, D_MODEL), row),
        out_shape=jax.ShapeDtypeStruct((t, D_MODEL), F32),
        scratch_shapes=[
            pltpu.VMEM((tm + 2 * BLOCK, KV_WIDTH), BF16),
            pltpu.VMEM((tm + 2 * BLOCK, KV_WIDTH), BF16),
            pltpu.VMEM((tm + 2 * HALO, CONV_WIDTH), F32),
            pltpu.VMEM((tm, CONV_WIDTH), F32),
            pltpu.VMEM((tm, D_MODEL), BF16),
        ],
        compiler_params=pltpu.CompilerParams(
            dimension_semantics=("parallel",), vmem_limit_bytes=VMEM_LIMIT_BYTES),
        name="mixer",
    )(sink, x, q, k, k, k, v, v, v, u, u, u, cw, cb, lg, lb, w_out)


def _mlp_kernel(x_ref, g_ref, wu_ref, wd_ref, fg_ref, o_ref, h_ref, *, final_norm):
    j = pl.program_id(1)

    @pl.when(j == 0)
    def _():
        x = x_ref[...]
        h_ref[...] = _rms_norm_f32(x, g_ref[...]).astype(BF16)
        o_ref[...] = x

    a = jnp.dot(h_ref[...], wu_ref[...], preferred_element_type=F32)
    a = jnp.square(jnp.maximum(a, 0.0)).astype(BF16)
    o_ref[...] += jnp.dot(a, wd_ref[...], preferred_element_type=F32)

    if final_norm:
        @pl.when(j == pl.num_programs(1) - 1)
        def _():
            o_ref[...] = _rms_norm_f32(o_ref[...], fg_ref[...])


def _mlp(x, g, w_up, w_down, fg, final_norm):
    t = x.shape[0]
    tm, tf = TM_MLP, TF_MLP
    kern = functools.partial(_mlp_kernel, final_norm=final_norm)
    return pl.pallas_call(
        kern,
        grid=(t // tm, D_FF // tf),
        in_specs=[
            pl.BlockSpec((tm, D_MODEL), lambda i, j: (i, 0)),
            pl.BlockSpec((1, D_MODEL), lambda i, j: (0, 0)),
            pl.BlockSpec((D_MODEL, tf), lambda i, j: (0, j)),
            pl.BlockSpec((tf, D_MODEL), lambda i, j: (j, 0)),
            pl.BlockSpec((1, D_MODEL), lambda i, j: (0, 0)),
        ],
        out_specs=pl.BlockSpec((tm, D_MODEL), lambda i, j: (i, 0)),
        out_shape=jax.ShapeDtypeStruct((t, D_MODEL), F32),
        scratch_shapes=[pltpu.VMEM((tm, D_MODEL), BF16)],
        compiler_params=pltpu.CompilerParams(
            dimension_semantics=("parallel", "arbitrary"), vmem_limit_bytes=VMEM_LIMIT_BYTES),
        name="mlp",
    )(x, g, w_up, w_down, fg)


def _rope_tables(seq):
    half = HEAD_DIM // 2
    inv_freq = jnp.exp(-math.log(ROPE_THETA) * jnp.arange(half, dtype=F32) / half)
    ang = jnp.arange(seq, dtype=F32)[:, None] * inv_freq[None, :]
    cos = jnp.cos(ang)
    sin = jnp.sin(ang)
    return jnp.concatenate([cos, cos], axis=-1), jnp.concatenate([-sin, sin], axis=-1)


def _trunk(x3, params, final_norm):
    b, seq, _ = x3.shape
    x = x3.reshape(b * seq, D_MODEL)
    cos, sin = _rope_tables(seq)
    depth = len(params["w_in"])
    fg = final_norm.reshape(1, D_MODEL)
    for l in range(depth):
        q, k, v, u = _inproj(x, params["norm_mix"][l], params["w_in"][l], cos, sin, seq)
        x = _mixer(x, q, k, v, u, params["attn_sink"][l], params["conv_w"][l], params["conv_b"][l],
                   params["conv_ln_g"][l], params["conv_ln_b"][l], params["w_out"][l], seq)
        x = _mlp(x, params["norm_ffn"][l], params["w_up"][l], params["w_down"][l], fg,
                 final_norm=(l == depth - 1))
    return x.reshape(b, seq, D_MODEL)


def kernel(x_prompt, x_sample, norm_mix, w_in, attn_sink, conv_w, conv_b, conv_ln_g, conv_ln_b,
           w_out, norm_ffn, w_up, w_down, final_norm):
    depth = w_in.shape[0]
    params = {
        "norm_mix": norm_mix.reshape(depth, 1, D_MODEL),
        "w_in": [w_in[l].astype(BF16) for l in range(depth)],
        "attn_sink": attn_sink,
        "conv_w": jnp.pad(conv_w, ((0, 0), (0, 1), (0, 0))),
        "conv_b": conv_b.reshape(depth, 1, CONV_WIDTH),
        "conv_ln_g": conv_ln_g.reshape(depth, 1, CONV_WIDTH),
        "conv_ln_b": conv_ln_b.reshape(depth, 1, CONV_WIDTH),
        "w_out": [w_out[l].astype(BF16) for l in range(depth)],
        "norm_ffn": norm_ffn.reshape(depth, 1, D_MODEL),
        "w_up": [w_up[l].astype(BF16) for l in range(depth)],
        "w_down": [w_down[l].astype(BF16) for l in range(depth)],
    }
    y_prompt = _trunk(x_prompt, params, final_norm)
    y_sample = _trunk(x_sample, params, final_norm)
    return (y_prompt, y_sample)
```

```python
import functools
import math

import jax
import jax.numpy as jnp
from jax import lax
from jax.experimental import pallas as pl
from jax.experimental.pallas import tpu as pltpu

D_MODEL = 2048
HEAD_DIM = 128
N_Q_HEADS = 8
N_KV_HEADS = 2
Q_PER_KV = N_Q_HEADS // N_KV_HEADS
ATTN_WIDTH = N_Q_HEADS * HEAD_DIM
CONV_WIDTH = D_MODEL - ATTN_WIDTH
KV_WIDTH = N_KV_HEADS * HEAD_DIM
IN_WIDTH = ATTN_WIDTH + 2 * KV_WIDTH + 2 * CONV_WIDTH
WINDOW = 128
BLOCK = 128
CONV_KERNEL = 31
CONV_PAD = CONV_KERNEL // 2
D_FF = 4 * D_MODEL
ROPE_THETA = 10000.0
NORM_EPS = 1e-5
MASK_VALUE = -1e30

V7X_VMEM_BYTES = 64 * 1024 * 1024
VMEM_LIMIT_BYTES = V7X_VMEM_BYTES - 8 * 1024 * 1024

TM_IN = 512
TM_MIX = 512
TM_MLP = 1024
TF_MLP = 512
HALO = 16
CONV_ROWS = 64
CONV_STRIDE = 4
LANES = 128
N_SLABS = CONV_WIDTH // LANES

BF16 = jnp.bfloat16
F32 = jnp.float32


def _rms_norm_f32(x, g):
    return x * lax.rsqrt(jnp.mean(x * x, axis=-1, keepdims=True) + NORM_EPS) * g


def _inproj_kernel(x_ref, g_ref, w_ref, cos_ref, sin_ref, q_ref, k_ref, v_ref, u_ref):
    h = _rms_norm_f32(x_ref[...], g_ref[...]).astype(BF16)
    cos = cos_ref[...]
    sin = sin_ref[...]

    def rope(t):
        return t * cos + pltpu.roll(t, HEAD_DIM // 2, 1) * sin

    o1 = ATTN_WIDTH
    o2 = o1 + KV_WIDTH
    o3 = o2 + KV_WIDTH
    o4 = o3 + CONV_WIDTH
    pq = jnp.dot(h, w_ref[:, 0:o1], preferred_element_type=F32)
    for hd in range(N_Q_HEADS):
        sl = slice(hd * HEAD_DIM, (hd + 1) * HEAD_DIM)
        q_ref[:, sl] = rope(pq[:, sl]).astype(BF16)
    pk = jnp.dot(h, w_ref[:, o1:o2], preferred_element_type=F32)
    for hd in range(N_KV_HEADS):
        sl = slice(hd * HEAD_DIM, (hd + 1) * HEAD_DIM)
        k_ref[:, sl] = rope(pk[:, sl]).astype(BF16)
    v_ref[...] = jnp.dot(h, w_ref[:, o2:o3], preferred_element_type=F32).astype(BF16)
    pa = jnp.dot(h, w_ref[:, o3:o4], preferred_element_type=F32)
    pg = jnp.dot(h, w_ref[:, o4:], preferred_element_type=F32)
    u_ref[...] = (pa * jax.nn.sigmoid(pg)).astype(BF16)


def _inproj(x, g, w_in, cos, sin, seq):
    t = x.shape[0]
    tm = TM_IN
    nseq_blk = seq // tm
    row = lambda i: (i, 0)
    const = lambda i: (0, 0)
    pos = lambda i: (i % nseq_blk, 0)
    return pl.pallas_call(
        _inproj_kernel,
        grid=(t // tm,),
        in_specs=[
            pl.BlockSpec((tm, D_MODEL), row),
            pl.BlockSpec((1, D_MODEL), const),
            pl.BlockSpec((D_MODEL, IN_WIDTH), const, pipeline_mode=pl.Buffered(1)),
            pl.BlockSpec((tm, HEAD_DIM), pos),
            pl.BlockSpec((tm, HEAD_DIM), pos),
        ],
        out_specs=[
            pl.BlockSpec((tm, ATTN_WIDTH), row),
            pl.BlockSpec((tm, KV_WIDTH), row),
            pl.BlockSpec((tm, KV_WIDTH), row),
            pl.BlockSpec((tm, CONV_WIDTH), row),
        ],
        out_shape=[
            jax.ShapeDtypeStruct((t, ATTN_WIDTH), BF16),
            jax.ShapeDtypeStruct((t, KV_WIDTH), BF16),
            jax.ShapeDtypeStruct((t, KV_WIDTH), BF16),
            jax.ShapeDtypeStruct((t, CONV_WIDTH), BF16),
        ],
        compiler_params=pltpu.CompilerParams(
            dimension_semantics=("parallel",), vmem_limit_bytes=VMEM_LIMIT_BYTES),
        name="inproj",
    )(x, g, w_in, cos, sin)


def _stage_conv_input(up_ref, uc_ref, un_ref, ubuf, first, last):
    tm = TM_MIX
    zero_halo = jnp.zeros((HALO, LANES), F32)
    for cc in range(N_SLABS):
        lanes = slice(cc * LANES, (cc + 1) * LANES)
        ubuf[cc, 0:HALO, :] = jnp.where(first, zero_halo, up_ref[:, lanes].astype(F32))
        ubuf[cc, HALO:HALO + tm, :] = uc_ref[:, lanes].astype(F32)
        ubuf[cc, HALO + tm:, :] = jnp.where(last, zero_halo, un_ref[:, lanes].astype(F32))


def _conv_rows(r0, cw_ref, cb_ref, ubuf, ybuf):
    per = CONV_ROWS // CONV_STRIDE
    off0 = HALO - CONV_PAD
    for cc in range(N_SLABS):
        lanes = slice(cc * LANES, (cc + 1) * LANES)
        bias = jnp.broadcast_to(cb_ref[:, lanes], (per, LANES))
        accs = [bias] * CONV_STRIDE
        for o in range(CONV_KERNEL + CONV_STRIDE - 1):
            rows = ubuf[cc, pl.ds(r0 + off0 + o, per, stride=CONV_STRIDE), :]
            for c in range(CONV_STRIDE):
                j = o - c
                if 0 <= j < CONV_KERNEL:
                    accs[c] = accs[c] + cw_ref[j:j + 1, lanes] * rows
        for c in range(CONV_STRIDE):
            ybuf[cc, pl.ds(r0 + c, per, stride=CONV_STRIDE), :] = accs[c]


def _ln_rows(r0, lg_ref, lb_ref, ybuf, cv_out):
    inv_n = 1.0 / CONV_WIDTH
    ys = [ybuf[cc, pl.ds(r0, CONV_ROWS), :] for cc in range(N_SLABS)]
    tot = ys[0]
    for cc in range(1, N_SLABS):
        tot = tot + ys[cc]
    mu = jnp.sum(tot, axis=-1, keepdims=True) * inv_n
    ycs = [y - mu for y in ys]
    sq = ycs[0] * ycs[0]
    for cc in range(1, N_SLABS):
        sq = sq + ycs[cc] * ycs[cc]
    rstd = lax.rsqrt(jnp.sum(sq, axis=-1, keepdims=True) * inv_n + NORM_EPS)
    for cc in range(N_SLABS):
        lanes = slice(cc * LANES, (cc + 1) * LANES)
        yn = ycs[cc] * rstd * lg_ref[:, lanes] + lb_ref[:, lanes]
        cv_out[pl.ds(r0, CONV_ROWS), lanes] = (yn * jax.nn.sigmoid(yn)).astype(BF16)


def _attention_block(b, kvh, sink_ref, q_ref, kbuf, vbuf, mix, first, last):
    nblk = TM_MIX // BLOCK
    r = lax.broadcasted_iota(jnp.int32, (BLOCK, 3 * BLOCK), 0)
    c = lax.broadcasted_iota(jnp.int32, (BLOCK, 3 * BLOCK), 1)
    lo = jnp.where(first & (b == 0), BLOCK, 0)
    hi = jnp.where(last & (b == nblk - 1), 2 * BLOCK, 3 * BLOCK)
    valid = (c >= r) & (c <= r + 2 * WINDOW) & (c >= lo) & (c < hi)
    row0 = b * BLOCK
    kv_cols = pl.ds(kvh * HEAD_DIM, HEAD_DIM)
    kwin = kbuf[pl.ds(row0, 3 * BLOCK), kv_cols]
    vwin = vbuf[pl.ds(row0, 3 * BLOCK), kv_cols]
    head_cols = [pl.ds((kvh * Q_PER_KV + g) * HEAD_DIM, HEAD_DIM) for g in range(Q_PER_KV)]
    qs = jnp.concatenate([q_ref[pl.ds(row0, BLOCK), cols] for cols in head_cols], axis=0)
    s = lax.dot_general(qs, kwin, (((1,), (1,)), ((), ())),
                        preferred_element_type=F32) * (HEAD_DIM ** -0.5)
    for g in range(Q_PER_KV):
        sg = jnp.where(valid, s[g * BLOCK:(g + 1) * BLOCK, :], MASK_VALUE)
        sink = sink_ref[kvh * Q_PER_KV + g]
        m = jnp.maximum(jnp.max(sg, axis=-1, keepdims=True), sink)
        p = jnp.exp(sg - m)
        denom = jnp.sum(p, axis=-1, keepdims=True) + jnp.exp(sink - m)
        pv = jnp.dot(p.astype(BF16), vwin, preferred_element_type=F32)
        mix[pl.ds(row0, BLOCK), head_cols[g]] = (pv / denom).astype(BF16)


def _mixer_kernel(sink_ref, x_ref, q_ref, kp_ref, kc_ref, kn_ref, vp_ref, vc_ref, vn_ref,
                  up_ref, uc_ref, un_ref, cw_ref, cb_ref, lg_ref, lb_ref, wo_ref,
                  o_ref, kbuf, vbuf, ubuf, ybuf, cvbuf, mix, *, tiles_per_seq, n_tiles):
    tm = TM_MIX
    i = pl.program_id(0)

    @pl.when(i == 0)
    def _():
        cvbuf[1] = jnp.zeros((tm, CONV_WIDTH), BF16)

    mix[:, ATTN_WIDTH:] = cvbuf[(i + 1) % 2]

    conv_tile = jnp.minimum(i, n_tiles - 1)
    _stage_conv_input(up_ref, uc_ref, un_ref, ubuf,
                      first=(conv_tile % tiles_per_seq) == 0,
                      last=(conv_tile % tiles_per_seq) == tiles_per_seq - 1)

    tile = jnp.maximum(i - 1, 0)
    first = (tile % tiles_per_seq) == 0
    last = (tile % tiles_per_seq) == tiles_per_seq - 1
    kbuf[0:BLOCK, :] = kp_ref[...]
    kbuf[BLOCK:BLOCK + tm, :] = kc_ref[...]
    kbuf[BLOCK + tm:, :] = kn_ref[...]
    vbuf[0:BLOCK, :] = vp_ref[...]
    vbuf[BLOCK:BLOCK + tm, :] = vc_ref[...]
    vbuf[BLOCK + tm:, :] = vn_ref[...]

    for b in range(tm // BLOCK):
        for kvh in range(N_KV_HEADS):
            _attention_block(b, kvh, sink_ref, q_ref, kbuf, vbuf, mix, first, last)

    cv_out = cvbuf.at[i % 2]
    n_chunk = D_MODEL // (tm // CONV_ROWS)

    for it in range(tm // CONV_ROWS):
        cols = slice(it * n_chunk, (it + 1) * n_chunk)
        o_ref[:, cols] = x_ref[:, cols] + jnp.dot(mix[...], wo_ref[:, cols],
                                                  preferred_element_type=F32)
        _conv_rows(it * CONV_ROWS, cw_ref, cb_ref, ubuf, ybuf)
        _ln_rows(it * CONV_ROWS, lg_ref, lb_ref, ybuf, cv_out)


def _mixer(x, q, k, v, u, sink, cw, cb, lg, lb, w_out, seq):
    t = x.shape[0]
    tm = TM_MIX
    n_tiles = t // tm
    tiles_per_seq = seq // tm
    bpt = tm // BLOCK
    hpt = tm // HALO
    nkb = t // BLOCK
    nhb = t // HALO
    tile = lambda i: jnp.maximum(i - 1, 0)
    ctile = lambda i: jnp.minimum(i, n_tiles - 1)
    row = lambda i: (tile(i), 0)
    const = lambda i: (0, 0)
    kprev = lambda i: (jnp.maximum(tile(i) * bpt - 1, 0), 0)
    knext = lambda i: (jnp.minimum((tile(i) + 1) * bpt, nkb - 1), 0)
    urow = lambda i: (ctile(i), 0)
    uprev = lambda i: (jnp.maximum(ctile(i) * hpt - 1, 0), 0)
    unext = lambda i: (jnp.minimum((ctile(i) + 1) * hpt, nhb - 1), 0)
    kern = functools.partial(_mixer_kernel, tiles_per_seq=tiles_per_seq, n_tiles=n_tiles)
    return pl.pallas_call(
        kern,
        grid=(n_tiles + 1,),
        in_specs=[
            pl.BlockSpec(memory_space=pltpu.SMEM),
            pl.BlockSpec((tm, D_MODEL), row),
            pl.BlockSpec((tm, ATTN_WIDTH), row),
            pl.BlockSpec((BLOCK, KV_WIDTH), kprev),
            pl.BlockSpec((tm, KV_WIDTH), row),
            pl.BlockSpec((BLOCK, KV_WIDTH), knext),
            pl.BlockSpec((BLOCK, KV_WIDTH), kprev),
            pl.BlockSpec((tm, KV_WIDTH), row),
            pl.BlockSpec((BLOCK, KV_WIDTH), knext),
            pl.BlockSpec((HALO, CONV_WIDTH), uprev),
            pl.BlockSpec((tm, CONV_WIDTH), urow),
            pl.BlockSpec((HALO, CONV_WIDTH), unext),
            pl.BlockSpec((CONV_KERNEL + 1, CONV_WIDTH), const),
            pl.BlockSpec((1, CONV_WIDTH), const),
            pl.BlockSpec((1, CONV_WIDTH), const),
            pl.BlockSpec((1, CONV_WIDTH), const),
            pl.BlockSpec((D_MODEL, D_MODEL), const, pipeline_mode=pl.Buffered(1)),
        ],
        out_specs=pl.BlockSpec((tm, D_MODEL), row),
        out_shape=jax.ShapeDtypeStruct((t, D_MODEL), F32),
        scratch_shapes=[
            pltpu.VMEM((tm + 2 * BLOCK, KV_WIDTH), BF16),
            pltpu.VMEM((tm + 2 * BLOCK, KV_WIDTH), BF16),
            pltpu.VMEM((N_SLABS, tm + 2 * HALO, LANES), F32),
            pltpu.VMEM((N_SLABS, tm, LANES), F32),
            pltpu.VMEM((2, tm, CONV_WIDTH), BF16),
            pltpu.VMEM((tm, D_MODEL), BF16),
        ],
        compiler_params=pltpu.CompilerParams(
            dimension_semantics=("arbitrary",), vmem_limit_bytes=VMEM_LIMIT_BYTES),
        name="mixer",
    )(sink, x, q, k, k, k, v, v, v, u, u, u, cw, cb, lg, lb, w_out)


def _mlp_kernel(x_ref, g_ref, wu_ref, wd_ref, fg_ref, o_ref, h_ref, *, final_norm):
    j = pl.program_id(1)

    @pl.when(j == 0)
    def _():
        x = x_ref[...]
        h_ref[...] = _rms_norm_f32(x, g_ref[...]).astype(BF16)
        o_ref[...] = x

    a = jnp.dot(h_ref[...], wu_ref[...], preferred_element_type=F32)
    a = jnp.square(jnp.maximum(a, 0.0)).astype(BF16)
    o_ref[...] += jnp.dot(a, wd_ref[...], preferred_element_type=F32)

    if final_norm:
        @pl.when(j == pl.num_programs(1) - 1)
        def _():
            o_ref[...] = _rms_norm_f32(o_ref[...], fg_ref[...])


def _mlp(x, g, w_up, w_down, fg, final_norm):
    t = x.shape[0]
    tm, tf = TM_MLP, TF_MLP
    kern = functools.partial(_mlp_kernel, final_norm=final_norm)
    return pl.pallas_call(
        kern,
        grid=(t // tm, D_FF // tf),
        in_specs=[
            pl.BlockSpec((tm, D_MODEL), lambda i, j: (i, 0)),
            pl.BlockSpec((1, D_MODEL), lambda i, j: (0, 0)),
            pl.BlockSpec((D_MODEL, tf), lambda i, j: (0, j)),
            pl.BlockSpec((tf, D_MODEL), lambda i, j: (j, 0)),
            pl.BlockSpec((1, D_MODEL), lambda i, j: (0, 0)),
        ],
        out_specs=pl.BlockSpec((tm, D_MODEL), lambda i, j: (i, 0)),
        out_shape=jax.ShapeDtypeStruct((t, D_MODEL), F32),
        scratch_shapes=[pltpu.VMEM((tm, D_MODEL), BF16)],
        compiler_params=pltpu.CompilerParams(
            dimension_semantics=("parallel", "arbitrary"), vmem_limit_bytes=VMEM_LIMIT_BYTES),
        name="mlp",
    )(x, g, w_up, w_down, fg)


def _rope_tables(seq):
    half = HEAD_DIM // 2
    inv_freq = jnp.exp(-math.log(ROPE_THETA) * jnp.arange(half, dtype=F32) / half)
    ang = jnp.arange(seq, dtype=F32)[:, None] * inv_freq[None, :]
    cos = jnp.cos(ang)
    sin = jnp.sin(ang)
    return jnp.concatenate([cos, cos], axis=-1), jnp.concatenate([-sin, sin], axis=-1)


def _trunk(x3, params, final_norm):
    b, seq, _ = x3.shape
    x = x3.reshape(b * seq, D_MODEL)
    cos, sin = _rope_tables(seq)
    depth = len(params["w_in"])
    fg = final_norm.reshape(1, D_MODEL)
    for l in range(depth):
        q, k, v, u = _inproj(x, params["norm_mix"][l], params["w_in"][l], cos, sin, seq)
        x = _mixer(x, q, k, v, u, params["attn_sink"][l], params["conv_w"][l], params["conv_b"][l],
                   params["conv_ln_g"][l], params["conv_ln_b"][l], params["w_out"][l], seq)
        x = _mlp(x, params["norm_ffn"][l], params["w_up"][l], params["w_down"][l], fg,
                 final_norm=(l == depth - 1))
    return x.reshape(b, seq, D_MODEL)


def kernel(x_prompt, x_sample, norm_mix, w_in, attn_sink, conv_w, conv_b, conv_ln_g, conv_ln_b,
           w_out, norm_ffn, w_up, w_down, final_norm):
    depth = w_in.shape[0]
    params = {
        "norm_mix": norm_mix.reshape(depth, 1, D_MODEL),
        "w_in": [w_in[l].astype(BF16) for l in range(depth)],
        "attn_sink": attn_sink,
        "conv_w": jnp.pad(conv_w, ((0, 0), (0, 1), (0, 0))),
        "conv_b": conv_b.reshape(depth, 1, CONV_WIDTH),
        "conv_ln_g": conv_ln_g.reshape(depth, 1, CONV_WIDTH),
        "conv_ln_b": conv_ln_b.reshape(depth, 1, CONV_WIDTH),
        "w_out": [w_out[l].astype(BF16) for l in range(depth)],
        "norm_ffn": norm_ffn.reshape(depth, 1, D_MODEL),
        "w_up": [w_up[l].astype(BF16) for l in range(depth)],
        "w_down": [w_down[l].astype(BF16) for l in range(depth)],
    }
    y_prompt = _trunk(x_prompt, params, final_norm)
    y_sample = _trunk(x_sample, params, final_norm)
    return (y_prompt, y_sample)
```

```python
import functools
import math

import jax
import jax.numpy as jnp
from jax import lax
from jax.experimental import pallas as pl
from jax.experimental.pallas import tpu as pltpu

D_MODEL = 2048
HEAD_DIM = 128
N_Q_HEADS = 8
N_KV_HEADS = 2
Q_PER_KV = N_Q_HEADS // N_KV_HEADS
ATTN_WIDTH = N_Q_HEADS * HEAD_DIM
CONV_WIDTH = D_MODEL - ATTN_WIDTH
KV_WIDTH = N_KV_HEADS * HEAD_DIM
IN_WIDTH = ATTN_WIDTH + 2 * KV_WIDTH + 2 * CONV_WIDTH
WINDOW = 128
BLOCK = 128
CONV_KERNEL = 31
CONV_PAD = CONV_KERNEL // 2
D_FF = 4 * D_MODEL
ROPE_THETA = 10000.0
NORM_EPS = 1e-5
MASK_VALUE = -1e30

V7X_VMEM_BYTES = 64 * 1024 * 1024
VMEM_LIMIT_BYTES = V7X_VMEM_BYTES - 8 * 1024 * 1024

TM_IN = 512
TM_MIX = 512
TM_MLP = 1024
TF_MLP = 1024
TF_CHUNK = 512
HALO = 16
CONV_ROWS = 64
CONV_STRIDE = 4
LANES = 128
BF16_SUBLANES = 16
N_SLABS = CONV_WIDTH // LANES

BF16 = jnp.bfloat16
F32 = jnp.float32


def _rms_norm_f32(x, g):
    return x * lax.rsqrt(jnp.mean(x * x, axis=-1, keepdims=True) + NORM_EPS) * g


def _inproj_kernel(x_ref, g_ref, w_ref, cos_ref, sin_ref, *refs, n_cast):
    cast_in, (q_ref, k_ref, v_ref, u_ref), cast_out = refs[:n_cast], refs[n_cast:n_cast + 4], refs[n_cast + 4:]
    for src, dst in zip(cast_in, cast_out):
        dst[...] = src[...].astype(BF16)

    h = _rms_norm_f32(x_ref[...], g_ref[...]).astype(BF16)
    cos = cos_ref[...]
    sin = sin_ref[...]

    def rope(t):
        return t * cos + pltpu.roll(t, HEAD_DIM // 2, 1) * sin

    o1 = ATTN_WIDTH
    o2 = o1 + KV_WIDTH
    o3 = o2 + KV_WIDTH
    o4 = o3 + CONV_WIDTH
    pq = jnp.dot(h, w_ref[:, 0:o1], preferred_element_type=F32)
    for hd in range(N_Q_HEADS):
        sl = slice(hd * HEAD_DIM, (hd + 1) * HEAD_DIM)
        q_ref[:, sl] = rope(pq[:, sl]).astype(BF16)
    pk = jnp.dot(h, w_ref[:, o1:o2], preferred_element_type=F32)
    for hd in range(N_KV_HEADS):
        sl = slice(hd * HEAD_DIM, (hd + 1) * HEAD_DIM)
        k_ref[:, sl] = rope(pk[:, sl]).astype(BF16)
    v_ref[...] = jnp.dot(h, w_ref[:, o2:o3], preferred_element_type=F32).astype(BF16)
    pa = jnp.dot(h, w_ref[:, o3:o4], preferred_element_type=F32)
    pg = jnp.dot(h, w_ref[:, o4:], preferred_element_type=F32)
    u_ref[...] = (pa * jax.nn.sigmoid(pg)).astype(BF16)


def _inproj(x, g, w_in, cos, sin, seq, cast=()):
    t = x.shape[0]
    tm = TM_IN
    n_steps = t // tm
    nseq_blk = seq // tm
    row = lambda i: (i, 0)
    const = lambda i: (0, 0)
    pos = lambda i: (i % nseq_blk, 0)
    cast_in_specs, cast_out_specs, cast_out_shapes = [], [], []
    for w, layer in cast:
        _, rows, cols = w.shape
        rps = rows // n_steps
        assert rps * n_steps == rows and rps % BF16_SUBLANES == 0
        cast_in_specs.append(pl.BlockSpec((None, rps, cols), lambda i, layer=layer: (layer, i, 0)))
        cast_out_specs.append(pl.BlockSpec((rps, cols), row))
        cast_out_shapes.append(jax.ShapeDtypeStruct((rows, cols), BF16))
    return pl.pallas_call(
        functools.partial(_inproj_kernel, n_cast=len(cast)),
        grid=(n_steps,),
        in_specs=[
            pl.BlockSpec((tm, D_MODEL), row),
            pl.BlockSpec((1, D_MODEL), const),
            pl.BlockSpec((D_MODEL, IN_WIDTH), const, pipeline_mode=pl.Buffered(1)),
            pl.BlockSpec((tm, HEAD_DIM), pos),
            pl.BlockSpec((tm, HEAD_DIM), pos),
        ] + cast_in_specs,
        out_specs=[
            pl.BlockSpec((tm, ATTN_WIDTH), row),
            pl.BlockSpec((tm, KV_WIDTH), row),
            pl.BlockSpec((tm, KV_WIDTH), row),
            pl.BlockSpec((tm, CONV_WIDTH), row),
        ] + cast_out_specs,
        out_shape=[
            jax.ShapeDtypeStruct((t, ATTN_WIDTH), BF16),
            jax.ShapeDtypeStruct((t, KV_WIDTH), BF16),
            jax.ShapeDtypeStruct((t, KV_WIDTH), BF16),
            jax.ShapeDtypeStruct((t, CONV_WIDTH), BF16),
        ] + cast_out_shapes,
        compiler_params=pltpu.CompilerParams(
            dimension_semantics=("parallel",), vmem_limit_bytes=VMEM_LIMIT_BYTES),
        name="inproj",
    )(x, g, w_in, cos, sin, *[w for w, _ in cast])


def _stage_conv_input(up_ref, uc_ref, un_ref, ubuf, first, last):
    tm = TM_MIX
    zero_halo = jnp.zeros((HALO, LANES), F32)
    for cc in range(N_SLABS):
        lanes = slice(cc * LANES, (cc + 1) * LANES)
        ubuf[cc, 0:HALO, :] = jnp.where(first, zero_halo, up_ref[:, lanes].astype(F32))
        ubuf[cc, HALO:HALO + tm, :] = uc_ref[:, lanes].astype(F32)
        ubuf[cc, HALO + tm:, :] = jnp.where(last, zero_halo, un_ref[:, lanes].astype(F32))


def _conv_rows(r0, cw_ref, cb_ref, ubuf, ybuf):
    per = CONV_ROWS // CONV_STRIDE
    off0 = HALO - CONV_PAD
    for cc in range(N_SLABS):
        lanes = slice(cc * LANES, (cc + 1) * LANES)
        bias = jnp.broadcast_to(cb_ref[:, lanes], (per, LANES))
        accs = [bias] * CONV_STRIDE
        for o in range(CONV_KERNEL + CONV_STRIDE - 1):
            rows = ubuf[cc, pl.ds(r0 + off0 + o, per, stride=CONV_STRIDE), :]
            for c in range(CONV_STRIDE):
                j = o - c
                if 0 <= j < CONV_KERNEL:
                    accs[c] = accs[c] + cw_ref[j:j + 1, lanes] * rows
        for c in range(CONV_STRIDE):
            ybuf[cc, pl.ds(r0 + c, per, stride=CONV_STRIDE), :] = accs[c]


def _ln_rows(r0, lg_ref, lb_ref, ybuf, cv_out):
    inv_n = 1.0 / CONV_WIDTH
    ys = [ybuf[cc, pl.ds(r0, CONV_ROWS), :] for cc in range(N_SLABS)]
    tot = ys[0]
    for cc in range(1, N_SLABS):
        tot = tot + ys[cc]
    mu = jnp.sum(tot, axis=-1, keepdims=True) * inv_n
    ycs = [y - mu for y in ys]
    sq = ycs[0] * ycs[0]
    for cc in range(1, N_SLABS):
        sq = sq + ycs[cc] * ycs[cc]
    rstd = lax.rsqrt(jnp.sum(sq, axis=-1, keepdims=True) * inv_n + NORM_EPS)
    for cc in range(N_SLABS):
        lanes = slice(cc * LANES, (cc + 1) * LANES)
        yn = ycs[cc] * rstd * lg_ref[:, lanes] + lb_ref[:, lanes]
        cv_out[pl.ds(r0, CONV_ROWS), lanes] = (yn * jax.nn.sigmoid(yn)).astype(BF16)


def _attention_block(b, kvh, sink_ref, q_ref, kbuf, vbuf, mix, first, last):
    nblk = TM_MIX // BLOCK
    r = lax.broadcasted_iota(jnp.int32, (BLOCK, 3 * BLOCK), 0)
    c = lax.broadcasted_iota(jnp.int32, (BLOCK, 3 * BLOCK), 1)
    lo = jnp.where(first & (b == 0), BLOCK, 0)
    hi = jnp.where(last & (b == nblk - 1), 2 * BLOCK, 3 * BLOCK)
    valid = (c >= r) & (c <= r + 2 * WINDOW) & (c >= lo) & (c < hi)
    row0 = b * BLOCK
    kv_cols = pl.ds(kvh * HEAD_DIM, HEAD_DIM)
    kwin = kbuf[pl.ds(row0, 3 * BLOCK), kv_cols]
    vwin = vbuf[pl.ds(row0, 3 * BLOCK), kv_cols]
    head_cols = [pl.ds((kvh * Q_PER_KV + g) * HEAD_DIM, HEAD_DIM) for g in range(Q_PER_KV)]
    qs = jnp.concatenate([q_ref[pl.ds(row0, BLOCK), cols] for cols in head_cols], axis=0)
    s = lax.dot_general(qs, kwin, (((1,), (1,)), ((), ())),
                        preferred_element_type=F32) * (HEAD_DIM ** -0.5)
    for g in range(Q_PER_KV):
        sg = jnp.where(valid, s[g * BLOCK:(g + 1) * BLOCK, :], MASK_VALUE)
        sink = sink_ref[kvh * Q_PER_KV + g]
        m = jnp.maximum(jnp.max(sg, axis=-1, keepdims=True), sink)
        p = jnp.exp(sg - m)
        denom = jnp.sum(p, axis=-1, keepdims=True) + jnp.exp(sink - m)
        pv = jnp.dot(p.astype(BF16), vwin, preferred_element_type=F32)
        mix[pl.ds(row0, BLOCK), head_cols[g]] = (pv / denom).astype(BF16)


def _mixer_kernel(sink_ref, x_ref, q_ref, kp_ref, kc_ref, kn_ref, vp_ref, vc_ref, vn_ref,
                  up_ref, uc_ref, un_ref, cw_ref, cb_ref, lg_ref, lb_ref, wo_ref,
                  o_ref, kbuf, vbuf, ubuf, ybuf, cvbuf, mix, *, tiles_per_seq, n_tiles):
    tm = TM_MIX
    i = pl.program_id(0)

    @pl.when(i == 0)
    def _():
        cvbuf[1] = jnp.zeros((tm, CONV_WIDTH), BF16)

    mix[:, ATTN_WIDTH:] = cvbuf[(i + 1) % 2]

    conv_tile = jnp.minimum(i, n_tiles - 1)
    _stage_conv_input(up_ref, uc_ref, un_ref, ubuf,
                      first=(conv_tile % tiles_per_seq) == 0,
                      last=(conv_tile % tiles_per_seq) == tiles_per_seq - 1)

    tile = jnp.maximum(i - 1, 0)
    first = (tile % tiles_per_seq) == 0
    last = (tile % tiles_per_seq) == tiles_per_seq - 1
    kbuf[0:BLOCK, :] = kp_ref[...]
    kbuf[BLOCK:BLOCK + tm, :] = kc_ref[...]
    kbuf[BLOCK + tm:, :] = kn_ref[...]
    vbuf[0:BLOCK, :] = vp_ref[...]
    vbuf[BLOCK:BLOCK + tm, :] = vc_ref[...]
    vbuf[BLOCK + tm:, :] = vn_ref[...]

    for b in range(tm // BLOCK):
        for kvh in range(N_KV_HEADS):
            _attention_block(b, kvh, sink_ref, q_ref, kbuf, vbuf, mix, first, last)

    cv_out = cvbuf.at[i % 2]
    n_chunk = D_MODEL // (tm // CONV_ROWS)

    for it in range(tm // CONV_ROWS):
        cols = slice(it * n_chunk, (it + 1) * n_chunk)
        o_ref[:, cols] = x_ref[:, cols] + jnp.dot(mix[...], wo_ref[:, cols],
                                                  preferred_element_type=F32)
        _conv_rows(it * CONV_ROWS, cw_ref, cb_ref, ubuf, ybuf)
        _ln_rows(it * CONV_ROWS, lg_ref, lb_ref, ybuf, cv_out)


def _mixer(x, q, k, v, u, sink, cw, cb, lg, lb, w_out, seq):
    t = x.shape[0]
    tm = TM_MIX
    n_tiles = t // tm
    tiles_per_seq = seq // tm
    bpt = tm // BLOCK
    hpt = tm // HALO
    nkb = t // BLOCK
    nhb = t // HALO
    tile = lambda i: jnp.maximum(i - 1, 0)
    ctile = lambda i: jnp.minimum(i, n_tiles - 1)
    row = lambda i: (tile(i), 0)
    const = lambda i: (0, 0)
    kprev = lambda i: (jnp.maximum(tile(i) * bpt - 1, 0), 0)
    knext = lambda i: (jnp.minimum((tile(i) + 1) * bpt, nkb - 1), 0)
    urow = lambda i: (ctile(i), 0)
    uprev = lambda i: (jnp.maximum(ctile(i) * hpt - 1, 0), 0)
    unext = lambda i: (jnp.minimum((ctile(i) + 1) * hpt, nhb - 1), 0)
    kern = functools.partial(_mixer_kernel, tiles_per_seq=tiles_per_seq, n_tiles=n_tiles)
    return pl.pallas_call(
        kern,
        grid=(n_tiles + 1,),
        in_specs=[
            pl.BlockSpec(memory_space=pltpu.SMEM),
            pl.BlockSpec((tm, D_MODEL), row),
            pl.BlockSpec((tm, ATTN_WIDTH), row),
            pl.BlockSpec((BLOCK, KV_WIDTH), kprev),
            pl.BlockSpec((tm, KV_WIDTH), row),
            pl.BlockSpec((BLOCK, KV_WIDTH), knext),
            pl.BlockSpec((BLOCK, KV_WIDTH), kprev),
            pl.BlockSpec((tm, KV_WIDTH), row),
            pl.BlockSpec((BLOCK, KV_WIDTH), knext),
            pl.BlockSpec((HALO, CONV_WIDTH), uprev),
            pl.BlockSpec((tm, CONV_WIDTH), urow),
            pl.BlockSpec((HALO, CONV_WIDTH), unext),
            pl.BlockSpec((CONV_KERNEL + 1, CONV_WIDTH), const),
            pl.BlockSpec((1, CONV_WIDTH), const),
            pl.BlockSpec((1, CONV_WIDTH), const),
            pl.BlockSpec((1, CONV_WIDTH), const),
            pl.BlockSpec((D_MODEL, D_MODEL), const, pipeline_mode=pl.Buffered(1)),
        ],
        out_specs=pl.BlockSpec((tm, D_MODEL), row),
        out_shape=jax.ShapeDtypeStruct((t, D_MODEL), F32),
        scratch_shapes=[
            pltpu.VMEM((tm + 2 * BLOCK, KV_WIDTH), BF16),
            pltpu.VMEM((tm + 2 * BLOCK, KV_WIDTH), BF16),
            pltpu.VMEM((N_SLABS, tm + 2 * HALO, LANES), F32),
            pltpu.VMEM((N_SLABS, tm, LANES), F32),
            pltpu.VMEM((2, tm, CONV_WIDTH), BF16),
            pltpu.VMEM((tm, D_MODEL), BF16),
        ],
        compiler_params=pltpu.CompilerParams(
            dimension_semantics=("arbitrary",), vmem_limit_bytes=VMEM_LIMIT_BYTES),
        name="mixer",
    )(sink, x, q, k, k, k, v, v, v, u, u, u, cw, cb, lg, lb, w_out)


def _mlp_kernel(x_ref, g_ref, wu_ref, wd_ref, fg_ref, o_ref, h_ref, *, final_norm):
    j = pl.program_id(1)

    @pl.when(j == 0)
    def _():
        x = x_ref[...]
        h_ref[...] = _rms_norm_f32(x, g_ref[...]).astype(BF16)
        o_ref[...] = x

    for c in range(TF_MLP // TF_CHUNK):
        cols = slice(c * TF_CHUNK, (c + 1) * TF_CHUNK)
        a = jnp.dot(h_ref[...], wu_ref[:, cols], preferred_element_type=F32)
        a = jnp.square(jnp.maximum(a, 0.0)).astype(BF16)
        o_ref[...] += jnp.dot(a, wd_ref[cols, :], preferred_element_type=F32)

    if final_norm:
        @pl.when(j == pl.num_programs(1) - 1)
        def _():
            o_ref[...] = _rms_norm_f32(o_ref[...], fg_ref[...])


def _mlp(x, g, w_up, w_down, fg, final_norm):
    t = x.shape[0]
    tm, tf = TM_MLP, TF_MLP
    kern = functools.partial(_mlp_kernel, final_norm=final_norm)
    return pl.pallas_call(
        kern,
        grid=(t // tm, D_FF // tf),
        in_specs=[
            pl.BlockSpec((tm, D_MODEL), lambda i, j: (i, 0)),
            pl.BlockSpec((1, D_MODEL), lambda i, j: (0, 0)),
            pl.BlockSpec((D_MODEL, tf), lambda i, j: (0, j)),
            pl.BlockSpec((tf, D_MODEL), lambda i, j: (j, 0)),
            pl.BlockSpec((1, D_MODEL), lambda i, j: (0, 0)),
        ],
        out_specs=pl.BlockSpec((tm, D_MODEL), lambda i, j: (i, 0)),
        out_shape=jax.ShapeDtypeStruct((t, D_MODEL), F32),
        scratch_shapes=[pltpu.VMEM((tm, D_MODEL), BF16)],
        compiler_params=pltpu.CompilerParams(
            dimension_semantics=("parallel", "arbitrary"), vmem_limit_bytes=VMEM_LIMIT_BYTES),
        name="mlp",
    )(x, g, w_up, w_down, fg)


def _rope_tables(seq):
    half = HEAD_DIM // 2
    inv_freq = jnp.exp(-math.log(ROPE_THETA) * jnp.arange(half, dtype=F32) / half)
    ang = jnp.arange(seq, dtype=F32)[:, None] * inv_freq[None, :]
    cos = jnp.cos(ang)
    sin = jnp.sin(ang)
    return jnp.concatenate([cos, cos], axis=-1), jnp.concatenate([-sin, sin], axis=-1)


def _trunk(x3, params, final_norm, f32_weights=None, bf16_weights=None):
    b, seq, _ = x3.shape
    x = x3.reshape(b * seq, D_MODEL)
    cos, sin = _rope_tables(seq)
    depth = params["norm_mix"].shape[0]
    fg = final_norm.reshape(1, D_MODEL)
    if f32_weights is not None:
        bf16_weights = {name: [None] * depth for name in ("w_in", "w_out", "w_up", "w_down")}
        bf16_weights["w_in"][0] = f32_weights["w_in"][0].astype(BF16)
    wb = bf16_weights
    for l in range(depth):
        cast = ()
        if f32_weights is not None:
            cast = [(f32_weights[name], l) for name in ("w_out", "w_up", "w_down")]
            if l + 1 < depth:
                cast.append((f32_weights["w_in"], l + 1))
        q, k, v, u, *converted = _inproj(x, params["norm_mix"][l], wb["w_in"][l], cos, sin, seq, cast)
        if f32_weights is not None:
            wb["w_out"][l], wb["w_up"][l], wb["w_down"][l] = converted[:3]
            if l + 1 < depth:
                wb["w_in"][l + 1] = converted[3]
        x = _mixer(x, q, k, v, u, params["attn_sink"][l], params["conv_w"][l], params["conv_b"][l],
                   params["conv_ln_g"][l], params["conv_ln_b"][l], wb["w_out"][l], seq)
        x = _mlp(x, params["norm_ffn"][l], wb["w_up"][l], wb["w_down"][l], fg,
                 final_norm=(l == depth - 1))
    return x.reshape(b, seq, D_MODEL), wb


def kernel(x_prompt, x_sample, norm_mix, w_in, attn_sink, conv_w, conv_b, conv_ln_g, conv_ln_b,
           w_out, norm_ffn, w_up, w_down, final_norm):
    depth = w_in.shape[0]
    params = {
        "norm_mix": norm_mix.reshape(depth, 1, D_MODEL),
        "attn_sink": attn_sink,
        "conv_w": jnp.pad(conv_w, ((0, 0), (0, 1), (0, 0))),
        "conv_b": conv_b.reshape(depth, 1, CONV_WIDTH),
        "conv_ln_g": conv_ln_g.reshape(depth, 1, CONV_WIDTH),
        "conv_ln_b": conv_ln_b.reshape(depth, 1, CONV_WIDTH),
        "norm_ffn": norm_ffn.reshape(depth, 1, D_MODEL),
    }
    f32_weights = {"w_in": w_in, "w_out": w_out, "w_up": w_up, "w_down": w_down}
    y_prompt, bf16_weights = _trunk(x_prompt, params, final_norm, f32_weights=f32_weights)
    y_sample, _ = _trunk(x_sample, params, final_norm, bf16_weights=bf16_weights)
    return (y_prompt, y_sample)
```

```python
import functools
import math

import jax
import jax.numpy as jnp
from jax import lax
from jax.experimental import pallas as pl
from jax.experimental.pallas import tpu as pltpu

D_MODEL = 2048
HEAD_DIM = 128
N_Q_HEADS = 8
N_KV_HEADS = 2
Q_PER_KV = N_Q_HEADS // N_KV_HEADS
ATTN_WIDTH = N_Q_HEADS * HEAD_DIM
CONV_WIDTH = D_MODEL - ATTN_WIDTH
KV_WIDTH = N_KV_HEADS * HEAD_DIM
IN_WIDTH = ATTN_WIDTH + 2 * KV_WIDTH + 2 * CONV_WIDTH
WINDOW = 128
BLOCK = 128
CONV_KERNEL = 31
CONV_PAD = CONV_KERNEL // 2
D_FF = 4 * D_MODEL
ROPE_THETA = 10000.0
NORM_EPS = 1e-5
MASK_VALUE = -1e30

V7X_VMEM_BYTES = 64 * 1024 * 1024
VMEM_LIMIT_BYTES = V7X_VMEM_BYTES - 8 * 1024 * 1024

TM_IN = 512
TM_MIX = 512
TM_MLP = 1024
TF_MLP = 1024
TF_CHUNK = 512
HALO = 16
CONV_ROWS = 64
CONV_STRIDE = 4
LANES = 128
BF16_SUBLANES = 16
MXU_K = 256
N_SLABS = CONV_WIDTH // LANES

BF16 = jnp.bfloat16
F32 = jnp.float32


def _rms_norm_f32(x, g):
    return x * lax.rsqrt(jnp.mean(x * x, axis=-1, keepdims=True) + NORM_EPS) * g


def _inproj_kernel(x_ref, g_ref, w_ref, cos_ref, sin_ref, *refs, n_cast):
    cast_in, (q_ref, k_ref, v_ref, u_ref), cast_out = refs[:n_cast], refs[n_cast:n_cast + 4], refs[n_cast + 4:]
    for src, dst in zip(cast_in, cast_out):
        dst[...] = src[...].astype(BF16)

    h = _rms_norm_f32(x_ref[...], g_ref[...]).astype(BF16)
    cos = cos_ref[...]
    sin = sin_ref[...]

    def rope(t):
        return t * cos + pltpu.roll(t, HEAD_DIM // 2, 1) * sin

    o1 = ATTN_WIDTH
    o2 = o1 + KV_WIDTH
    o3 = o2 + KV_WIDTH
    o4 = o3 + CONV_WIDTH
    pq = jnp.dot(h, w_ref[:, 0:o1], preferred_element_type=F32)
    for hd in range(N_Q_HEADS):
        sl = slice(hd * HEAD_DIM, (hd + 1) * HEAD_DIM)
        q_ref[:, sl] = rope(pq[:, sl]).astype(BF16)
    pk = jnp.dot(h, w_ref[:, o1:o2], preferred_element_type=F32)
    for hd in range(N_KV_HEADS):
        sl = slice(hd * HEAD_DIM, (hd + 1) * HEAD_DIM)
        k_ref[:, sl] = rope(pk[:, sl]).astype(BF16)
    v_ref[...] = jnp.dot(h, w_ref[:, o2:o3], preferred_element_type=F32).astype(BF16)
    pa = jnp.dot(h, w_ref[:, o3:o4], preferred_element_type=F32)
    pg = jnp.dot(h, w_ref[:, o4:], preferred_element_type=F32)
    u_ref[...] = (pa * jax.nn.sigmoid(pg)).astype(BF16)


def _inproj(x, g, w_in, cos, sin, seq, cast=()):
    t = x.shape[0]
    tm = TM_IN
    n_steps = t // tm
    nseq_blk = seq // tm
    row = lambda i: (i, 0)
    const = lambda i: (0, 0)
    pos = lambda i: (i % nseq_blk, 0)
    cast_in_specs, cast_out_specs, cast_out_shapes = [], [], []
    for w, layer in cast:
        _, rows, cols = w.shape
        rps = rows // n_steps
        assert rps * n_steps == rows and rps % BF16_SUBLANES == 0
        cast_in_specs.append(pl.BlockSpec((None, rps, cols), lambda i, layer=layer: (layer, i, 0)))
        cast_out_specs.append(pl.BlockSpec((rps, cols), row))
        cast_out_shapes.append(jax.ShapeDtypeStruct((rows, cols), BF16))
    return pl.pallas_call(
        functools.partial(_inproj_kernel, n_cast=len(cast)),
        grid=(n_steps,),
        in_specs=[
            pl.BlockSpec((tm, D_MODEL), row),
            pl.BlockSpec((1, D_MODEL), const),
            pl.BlockSpec((D_MODEL, IN_WIDTH), const, pipeline_mode=pl.Buffered(1)),
            pl.BlockSpec((tm, HEAD_DIM), pos),
            pl.BlockSpec((tm, HEAD_DIM), pos),
        ] + cast_in_specs,
        out_specs=[
            pl.BlockSpec((tm, ATTN_WIDTH), row),
            pl.BlockSpec((tm, KV_WIDTH), row),
            pl.BlockSpec((tm, KV_WIDTH), row),
            pl.BlockSpec((tm, CONV_WIDTH), row),
        ] + cast_out_specs,
        out_shape=[
            jax.ShapeDtypeStruct((t, ATTN_WIDTH), BF16),
            jax.ShapeDtypeStruct((t, KV_WIDTH), BF16),
            jax.ShapeDtypeStruct((t, KV_WIDTH), BF16),
            jax.ShapeDtypeStruct((t, CONV_WIDTH), BF16),
        ] + cast_out_shapes,
        compiler_params=pltpu.CompilerParams(
            dimension_semantics=("parallel",), vmem_limit_bytes=VMEM_LIMIT_BYTES),
        name="inproj",
    )(x, g, w_in, cos, sin, *[w for w, _ in cast])


def _stage_conv_input(up_ref, uc_ref, un_ref, ubuf, first, last):
    tm = TM_MIX
    zero_halo = jnp.zeros((HALO, LANES), F32)
    for cc in range(N_SLABS):
        lanes = slice(cc * LANES, (cc + 1) * LANES)
        ubuf[cc, 0:HALO, :] = jnp.where(first, zero_halo, up_ref[:, lanes].astype(F32))
        ubuf[cc, HALO:HALO + tm, :] = uc_ref[:, lanes].astype(F32)
        ubuf[cc, HALO + tm:, :] = jnp.where(last, zero_halo, un_ref[:, lanes].astype(F32))


def _conv_rows(r0, cw_ref, cb_ref, ubuf, ybuf):
    per = CONV_ROWS // CONV_STRIDE
    off0 = HALO - CONV_PAD
    for cc in range(N_SLABS):
        lanes = slice(cc * LANES, (cc + 1) * LANES)
        bias = jnp.broadcast_to(cb_ref[:, lanes], (per, LANES))
        accs = [bias] * CONV_STRIDE
        for o in range(CONV_KERNEL + CONV_STRIDE - 1):
            rows = ubuf[cc, pl.ds(r0 + off0 + o, per, stride=CONV_STRIDE), :]
            for c in range(CONV_STRIDE):
                j = o - c
                if 0 <= j < CONV_KERNEL:
                    accs[c] = accs[c] + cw_ref[j:j + 1, lanes] * rows
        for c in range(CONV_STRIDE):
            ybuf[cc, pl.ds(r0 + c, per, stride=CONV_STRIDE), :] = accs[c]


def _ln_rows(r0, lg_ref, lb_ref, ybuf, mix):
    inv_n = 1.0 / CONV_WIDTH
    ys = [ybuf[cc, pl.ds(r0, CONV_ROWS), :] for cc in range(N_SLABS)]
    tot = ys[0]
    for cc in range(1, N_SLABS):
        tot = tot + ys[cc]
    mu = jnp.sum(tot, axis=-1, keepdims=True) * inv_n
    ycs = [y - mu for y in ys]
    sq = ycs[0] * ycs[0]
    for cc in range(1, N_SLABS):
        sq = sq + ycs[cc] * ycs[cc]
    rstd = lax.rsqrt(jnp.sum(sq, axis=-1, keepdims=True) * inv_n + NORM_EPS)
    for cc in range(N_SLABS):
        lanes = slice(cc * LANES, (cc + 1) * LANES)
        yn = ycs[cc] * rstd * lg_ref[:, lanes] + lb_ref[:, lanes]
        out_lanes = slice(ATTN_WIDTH + cc * LANES, ATTN_WIDTH + (cc + 1) * LANES)
        mix[pl.ds(r0, CONV_ROWS), out_lanes] = (yn * jax.nn.sigmoid(yn)).astype(BF16)


def _attention_block(b, kvh, sink_ref, q_ref, kbuf, vbuf, mix, first, last):
    nblk = TM_MIX // BLOCK
    r = lax.broadcasted_iota(jnp.int32, (BLOCK, 3 * BLOCK), 0)
    c = lax.broadcasted_iota(jnp.int32, (BLOCK, 3 * BLOCK), 1)
    lo = jnp.where(first & (b == 0), BLOCK, 0)
    hi = jnp.where(last & (b == nblk - 1), 2 * BLOCK, 3 * BLOCK)
    valid = (c >= r) & (c <= r + 2 * WINDOW) & (c >= lo) & (c < hi)
    row0 = b * BLOCK
    kv_cols = pl.ds(kvh * HEAD_DIM, HEAD_DIM)
    kwin = kbuf[pl.ds(row0, 3 * BLOCK), kv_cols]
    vwin = vbuf[pl.ds(row0, 3 * BLOCK), kv_cols]
    head_cols = [pl.ds((kvh * Q_PER_KV + g) * HEAD_DIM, HEAD_DIM) for g in range(Q_PER_KV)]
    qs = jnp.concatenate([q_ref[pl.ds(row0, BLOCK), cols] for cols in head_cols], axis=0)
    s = lax.dot_general(qs, kwin, (((1,), (1,)), ((), ())),
                        preferred_element_type=F32) * (HEAD_DIM ** -0.5)
    for g in range(Q_PER_KV):
        sg = jnp.where(valid, s[g * BLOCK:(g + 1) * BLOCK, :], MASK_VALUE)
        sink = sink_ref[kvh * Q_PER_KV + g]
        m = jnp.maximum(jnp.max(sg, axis=-1, keepdims=True), sink)
        p = jnp.exp(sg - m)
        denom = jnp.sum(p, axis=-1, keepdims=True) + jnp.exp(sink - m)
        pv = jnp.dot(p.astype(BF16), vwin, preferred_element_type=F32)
        mix[pl.ds(row0, BLOCK), head_cols[g]] = (pv / denom).astype(BF16)


def _mixer_kernel(sink_ref, x_ref, q_ref, kp_ref, kc_ref, kn_ref, vp_ref, vc_ref, vn_ref,
                  up_ref, uc_ref, un_ref, cw_ref, cb_ref, lg_ref, lb_ref, wo_ref,
                  o_ref, kbuf, vbuf, ubuf, ybuf, mix, *, tiles_per_seq):
    tm = TM_MIX
    i = pl.program_id(0)
    first = (i % tiles_per_seq) == 0
    last = (i % tiles_per_seq) == tiles_per_seq - 1

    _stage_conv_input(up_ref, uc_ref, un_ref, ubuf, first, last)
    for r0 in range(0, tm, CONV_ROWS):
        _conv_rows(r0, cw_ref, cb_ref, ubuf, ybuf)
        _ln_rows(r0, lg_ref, lb_ref, ybuf, mix)

    kbuf[0:BLOCK, :] = kp_ref[...]
    kbuf[BLOCK:BLOCK + tm, :] = kc_ref[...]
    kbuf[BLOCK + tm:, :] = kn_ref[...]
    vbuf[0:BLOCK, :] = vp_ref[...]
    vbuf[BLOCK:BLOCK + tm, :] = vc_ref[...]
    vbuf[BLOCK + tm:, :] = vn_ref[...]

    for b in range(tm // BLOCK):
        for kvh in range(N_KV_HEADS):
            _attention_block(b, kvh, sink_ref, q_ref, kbuf, vbuf, mix, first, last)

    o_ref[...] = x_ref[...] + jnp.dot(mix[...], wo_ref[...], preferred_element_type=F32)


def _mixer(x, q, k, v, u, sink, cw, cb, lg, lb, w_out, seq):
    t = x.shape[0]
    tm = TM_MIX
    tiles_per_seq = seq // tm
    bpt = tm // BLOCK
    hpt = tm // HALO
    nkb = t // BLOCK
    nhb = t // HALO
    row = lambda i: (i, 0)
    const = lambda i: (0, 0)
    kprev = lambda i: (jnp.maximum(i * bpt - 1, 0), 0)
    knext = lambda i: (jnp.minimum((i + 1) * bpt, nkb - 1), 0)
    uprev = lambda i: (jnp.maximum(i * hpt - 1, 0), 0)
    unext = lambda i: (jnp.minimum((i + 1) * hpt, nhb - 1), 0)
    kern = functools.partial(_mixer_kernel, tiles_per_seq=tiles_per_seq)
    return pl.pallas_call(
        kern,
        grid=(t // tm,),
        in_specs=[
            pl.BlockSpec(memory_space=pltpu.SMEM),
            pl.BlockSpec((tm, D_MODEL), row),
            pl.BlockSpec((tm, ATTN_WIDTH), row),
            pl.BlockSpec((BLOCK, KV_WIDTH), kprev),
            pl.BlockSpec((tm, KV_WIDTH), row),
            pl.BlockSpec((BLOCK, KV_WIDTH), knext),
            pl.BlockSpec((BLOCK, KV_WIDTH), kprev),
            pl.BlockSpec((tm, KV_WIDTH), row),
            pl.BlockSpec((BLOCK, KV_WIDTH), knext),
            pl.BlockSpec((HALO, CONV_WIDTH), uprev),
            pl.BlockSpec((tm, CONV_WIDTH), row),
            pl.BlockSpec((HALO, CONV_WIDTH), unext),
            pl.BlockSpec((CONV_KERNEL + 1, CONV_WIDTH), const),
            pl.BlockSpec((1, CONV_WIDTH), const),
            pl.BlockSpec((1, CONV_WIDTH), const),
            pl.BlockSpec((1, CONV_WIDTH), const),
            pl.BlockSpec((D_MODEL, D_MODEL), const, pipeline_mode=pl.Buffered(1)),
        ],
        out_specs=pl.BlockSpec((tm, D_MODEL), row),
        out_shape=jax.ShapeDtypeStruct((t, D_MODEL), F32),
        scratch_shapes=[
            pltpu.VMEM((tm + 2 * BLOCK, KV_WIDTH), BF16),
            pltpu.VMEM((tm + 2 * BLOCK, KV_WIDTH), BF16),
            pltpu.VMEM((N_SLABS, tm + 2 * HALO, LANES), F32),
            pltpu.VMEM((N_SLABS, tm, LANES), F32),
            pltpu.VMEM((tm, D_MODEL), BF16),
        ],
        compiler_params=pltpu.CompilerParams(
            dimension_semantics=("parallel",), vmem_limit_bytes=VMEM_LIMIT_BYTES),
        name="mixer",
    )(sink, x, q, k, k, k, v, v, v, u, u, u, cw, cb, lg, lb, w_out)


def _mlp_kernel(x_ref, g_ref, wu_ref, wd_ref, fg_ref, o_ref, h_ref, *, final_norm):
    j = pl.program_id(1)

    def mlp_block(first):
        for c in range(TF_MLP // TF_CHUNK):
            cols = slice(c * TF_CHUNK, (c + 1) * TF_CHUNK)
            a = jnp.dot(h_ref[...], wu_ref[:, cols], preferred_element_type=F32)
            a = jnp.square(jnp.maximum(a, 0.0)).astype(BF16)
            d = jnp.dot(a, wd_ref[cols, :], preferred_element_type=F32)
            if first and c == 0:
                o_ref[...] = x_ref[...] + d
            else:
                o_ref[...] += d

    @pl.when(j == 0)
    def _():
        x = x_ref[...]
        rstd = lax.rsqrt(jnp.mean(x * x, axis=-1, keepdims=True) + NORM_EPS)
        for k0 in range(0, D_MODEL, MXU_K):
            cols = slice(k0, k0 + MXU_K)
            h_ref[:, cols] = (x_ref[:, cols] * rstd * g_ref[:, cols]).astype(BF16)
        mlp_block(first=True)

    @pl.when(j > 0)
    def _():
        mlp_block(first=False)

    if final_norm:
        @pl.when(j == pl.num_programs(1) - 1)
        def _():
            o_ref[...] = _rms_norm_f32(o_ref[...], fg_ref[...])


def _mlp(x, g, w_up, w_down, fg, final_norm):
    t = x.shape[0]
    tm, tf = TM_MLP, TF_MLP
    kern = functools.partial(_mlp_kernel, final_norm=final_norm)
    return pl.pallas_call(
        kern,
        grid=(t // tm, D_FF // tf),
        in_specs=[
            pl.BlockSpec((tm, D_MODEL), lambda i, j: (i, 0)),
            pl.BlockSpec((1, D_MODEL), lambda i, j: (0, 0)),
            pl.BlockSpec((D_MODEL, tf), lambda i, j: (0, j)),
            pl.BlockSpec((tf, D_MODEL), lambda i, j: (j, 0)),
            pl.BlockSpec((1, D_MODEL), lambda i, j: (0, 0)),
        ],
        out_specs=pl.BlockSpec((tm, D_MODEL), lambda i, j: (i, 0)),
        out_shape=jax.ShapeDtypeStruct((t, D_MODEL), F32),
        scratch_shapes=[pltpu.VMEM((tm, D_MODEL), BF16)],
        compiler_params=pltpu.CompilerParams(
            dimension_semantics=("parallel", "arbitrary"), vmem_limit_bytes=VMEM_LIMIT_BYTES),
        name="mlp",
    )(x, g, w_up, w_down, fg)


def _rope_tables(seq):
    half = HEAD_DIM // 2
    inv_freq = jnp.exp(-math.log(ROPE_THETA) * jnp.arange(half, dtype=F32) / half)
    ang = jnp.arange(seq, dtype=F32)[:, None] * inv_freq[None, :]
    cos = jnp.cos(ang)
    sin = jnp.sin(ang)
    return jnp.concatenate([cos, cos], axis=-1), jnp.concatenate([-sin, sin], axis=-1)


def _trunk(x3, params, final_norm, f32_weights=None, bf16_weights=None):
    b, seq, _ = x3.shape
    x = x3.reshape(b * seq, D_MODEL)
    cos, sin = _rope_tables(seq)
    depth = params["norm_mix"].shape[0]
    fg = final_norm.reshape(1, D_MODEL)
    if f32_weights is not None:
        bf16_weights = {name: [None] * depth for name in ("w_in", "w_out", "w_up", "w_down")}
        bf16_weights["w_in"][0] = f32_weights["w_in"][0].astype(BF16)
    wb = bf16_weights
    for l in range(depth):
        cast = ()
        if f32_weights is not None:
            cast = [(f32_weights[name], l) for name in ("w_out", "w_up", "w_down")]
            if l + 1 < depth:
                cast.append((f32_weights["w_in"], l + 1))
        q, k, v, u, *converted = _inproj(x, params["norm_mix"][l], wb["w_in"][l], cos, sin, seq, cast)
        if f32_weights is not None:
            wb["w_out"][l], wb["w_up"][l], wb["w_down"][l] = converted[:3]
            if l + 1 < depth:
                wb["w_in"][l + 1] = converted[3]
        x = _mixer(x, q, k, v, u, params["attn_sink"][l], params["conv_w"][l], params["conv_b"][l],
                   params["conv_ln_g"][l], params["conv_ln_b"][l], wb["w_out"][l], seq)
        x = _mlp(x, params["norm_ffn"][l], wb["w_up"][l], wb["w_down"][l], fg,
                 final_norm=(l == depth - 1))
    return x.reshape(b, seq, D_MODEL), wb


def kernel(x_prompt, x_sample, norm_mix, w_in, attn_sink, conv_w, conv_b, conv_ln_g, conv_ln_b,
           w_out, norm_ffn, w_up, w_down, final_norm):
    depth = w_in.shape[0]
    params = {
        "norm_mix": norm_mix.reshape(depth, 1, D_MODEL),
        "attn_sink": attn_sink,
        "conv_w": jnp.pad(conv_w, ((0, 0), (0, 1), (0, 0))),
        "conv_b": conv_b.reshape(depth, 1, CONV_WIDTH),
        "conv_ln_g": conv_ln_g.reshape(depth, 1, CONV_WIDTH),
        "conv_ln_b": conv_ln_b.reshape(depth, 1, CONV_WIDTH),
        "norm_ffn": norm_ffn.reshape(depth, 1, D_MODEL),
    }
    f32_weights = {"w_in": w_in, "w_out": w_out, "w_up": w_up, "w_down": w_down}
    y_prompt, bf16_weights = _trunk(x_prompt, params, final_norm, f32_weights=f32_weights)
    y_sample, _ = _trunk(x_sample, params, final_norm, bf16_weights=bf16_weights)
    return (y_prompt, y_sample)
```

```python
import functools
import math

import jax
import jax.numpy as jnp
from jax import lax
from jax.experimental import pallas as pl
from jax.experimental.pallas import tpu as pltpu

D_MODEL = 2048
HEAD_DIM = 128
N_Q_HEADS = 8
N_KV_HEADS = 2
Q_PER_KV = N_Q_HEADS // N_KV_HEADS
ATTN_WIDTH = N_Q_HEADS * HEAD_DIM
CONV_WIDTH = D_MODEL - ATTN_WIDTH
KV_WIDTH = N_KV_HEADS * HEAD_DIM
IN_WIDTH = ATTN_WIDTH + 2 * KV_WIDTH + 2 * CONV_WIDTH
WINDOW = 128
BLOCK = 128
CONV_KERNEL = 31
CONV_PAD = CONV_KERNEL // 2
D_FF = 4 * D_MODEL
ROPE_THETA = 10000.0
NORM_EPS = 1e-5
MASK_VALUE = -1e30

V7X_VMEM_BYTES = 64 * 1024 * 1024
VMEM_LIMIT_BYTES = V7X_VMEM_BYTES - 8 * 1024 * 1024

TM_IN = 512
TM_MIX = 512
TM_MLP = 1024
TF_MLP = 1024
TF_CHUNK = 512
HALO = 16
CONV_ROWS = 64
CONV_STRIDE = 4
LANES = 128
BF16_SUBLANES = 16
MXU_K = 256
N_SLABS = CONV_WIDTH // LANES

BF16 = jnp.bfloat16
F32 = jnp.float32


def _rms_norm_f32(x, g):
    return x * lax.rsqrt(jnp.mean(x * x, axis=-1, keepdims=True) + NORM_EPS) * g


def _inproj_kernel(x_ref, xp_ref, xn_ref, g_ref, w_ref, cos_ref, sin_ref,
                   cw_ref, cb_ref, lg_ref, lb_ref, *refs, n_cast, tiles_per_seq):
    tm = TM_IN
    cast_in = refs[:n_cast]
    q_ref, k_ref, v_ref, cv_ref = refs[n_cast:n_cast + 4]
    cast_out = refs[n_cast + 4:2 * n_cast + 4]
    h_scr, ubuf, ybuf = refs[2 * n_cast + 4:]

    for src, dst in zip(cast_in, cast_out):
        dst[...] = src[...].astype(BF16)

    i = pl.program_id(0)
    first = (i % tiles_per_seq) == 0
    last = (i % tiles_per_seq) == tiles_per_seq - 1

    for src, r0, rows in ((xp_ref, 0, HALO), (x_ref, HALO, tm), (xn_ref, HALO + tm, HALO)):
        xs = src[...]
        rstd = lax.rsqrt(jnp.mean(xs * xs, axis=-1, keepdims=True) + NORM_EPS)
        for k0 in range(0, D_MODEL, MXU_K):
            cols = slice(k0, k0 + MXU_K)
            h_scr[r0:r0 + rows, cols] = (src[:, cols] * rstd * g_ref[:, cols]).astype(BF16)

    o1 = ATTN_WIDTH
    o2 = o1 + KV_WIDTH
    o3 = o2 + KV_WIDTH
    o4 = o3 + CONV_WIDTH

    h_ext = h_scr[...]
    pa = jnp.dot(h_ext, w_ref[:, o3:o4], preferred_element_type=F32)
    pg = jnp.dot(h_ext, w_ref[:, o4:], preferred_element_type=F32)
    u = pa * jax.nn.sigmoid(pg)
    zero_halo = jnp.zeros((HALO, LANES), F32)
    for cc in range(N_SLABS):
        lanes = slice(cc * LANES, (cc + 1) * LANES)
        ubuf[cc, 0:HALO, :] = jnp.where(first, zero_halo, u[0:HALO, lanes])
        ubuf[cc, HALO:HALO + tm, :] = u[HALO:HALO + tm, lanes]
        ubuf[cc, HALO + tm:, :] = jnp.where(last, zero_halo, u[HALO + tm:, lanes])
    for r0 in range(0, tm, CONV_ROWS):
        _conv_rows(r0, cw_ref, cb_ref, ubuf, ybuf)
        _ln_rows(r0, lg_ref, lb_ref, ybuf, cv_ref, 0)

    cos = cos_ref[...]
    sin = sin_ref[...]

    def rope(t):
        return t * cos + pltpu.roll(t, HEAD_DIM // 2, 1) * sin

    h = h_scr[HALO:HALO + tm, :]
    pq = jnp.dot(h, w_ref[:, 0:o1], preferred_element_type=F32)
    for hd in range(N_Q_HEADS):
        sl = slice(hd * HEAD_DIM, (hd + 1) * HEAD_DIM)
        q_ref[:, sl] = rope(pq[:, sl]).astype(BF16)
    pk = jnp.dot(h, w_ref[:, o1:o2], preferred_element_type=F32)
    for hd in range(N_KV_HEADS):
        sl = slice(hd * HEAD_DIM, (hd + 1) * HEAD_DIM)
        k_ref[:, sl] = rope(pk[:, sl]).astype(BF16)
    v_ref[...] = jnp.dot(h, w_ref[:, o2:o3], preferred_element_type=F32).astype(BF16)


def _inproj(x, g, w_in, cos, sin, cw, cb, lg, lb, seq, cast=()):
    t = x.shape[0]
    tm = TM_IN
    n_steps = t // tm
    nseq_blk = seq // tm
    hpt = tm // HALO
    nhb = t // HALO
    row = lambda i: (i, 0)
    const = lambda i: (0, 0)
    pos = lambda i: (i % nseq_blk, 0)
    xprev = lambda i: (jnp.maximum(i * hpt - 1, 0), 0)
    xnext = lambda i: (jnp.minimum((i + 1) * hpt, nhb - 1), 0)
    cast_in_specs, cast_out_specs, cast_out_shapes = [], [], []
    for w, layer in cast:
        _, rows, cols = w.shape
        rps = rows // n_steps
        assert rps * n_steps == rows and rps % BF16_SUBLANES == 0
        cast_in_specs.append(pl.BlockSpec((None, rps, cols), lambda i, layer=layer: (layer, i, 0)))
        cast_out_specs.append(pl.BlockSpec((rps, cols), row))
        cast_out_shapes.append(jax.ShapeDtypeStruct((rows, cols), BF16))
    return pl.pallas_call(
        functools.partial(_inproj_kernel, n_cast=len(cast), tiles_per_seq=nseq_blk),
        grid=(n_steps,),
        in_specs=[
            pl.BlockSpec((tm, D_MODEL), row),
            pl.BlockSpec((HALO, D_MODEL), xprev),
            pl.BlockSpec((HALO, D_MODEL), xnext),
            pl.BlockSpec((1, D_MODEL), const),
            pl.BlockSpec((D_MODEL, IN_WIDTH), const, pipeline_mode=pl.Buffered(1)),
            pl.BlockSpec((tm, HEAD_DIM), pos),
            pl.BlockSpec((tm, HEAD_DIM), pos),
            pl.BlockSpec((CONV_KERNEL + 1, CONV_WIDTH), const),
            pl.BlockSpec((1, CONV_WIDTH), const),
            pl.BlockSpec((1, CONV_WIDTH), const),
            pl.BlockSpec((1, CONV_WIDTH), const),
        ] + cast_in_specs,
        out_specs=[
            pl.BlockSpec((tm, ATTN_WIDTH), row),
            pl.BlockSpec((tm, KV_WIDTH), row),
            pl.BlockSpec((tm, KV_WIDTH), row),
            pl.BlockSpec((tm, CONV_WIDTH), row),
        ] + cast_out_specs,
        out_shape=[
            jax.ShapeDtypeStruct((t, ATTN_WIDTH), BF16),
            jax.ShapeDtypeStruct((t, KV_WIDTH), BF16),
            jax.ShapeDtypeStruct((t, KV_WIDTH), BF16),
            jax.ShapeDtypeStruct((t, CONV_WIDTH), BF16),
        ] + cast_out_shapes,
        scratch_shapes=[
            pltpu.VMEM((tm + 2 * HALO, D_MODEL), BF16),
            pltpu.VMEM((N_SLABS, tm + 2 * HALO, LANES), F32),
            pltpu.VMEM((N_SLABS, tm, LANES), F32),
        ],
        compiler_params=pltpu.CompilerParams(
            dimension_semantics=("parallel",), vmem_limit_bytes=VMEM_LIMIT_BYTES),
        name="inproj",
    )(x, x, x, g, w_in, cos, sin, cw, cb, lg, lb, *[w for w, _ in cast])


def _conv_rows(r0, cw_ref, cb_ref, ubuf, ybuf):
    per = CONV_ROWS // CONV_STRIDE
    off0 = HALO - CONV_PAD
    for cc in range(N_SLABS):
        lanes = slice(cc * LANES, (cc + 1) * LANES)
        bias = jnp.broadcast_to(cb_ref[:, lanes], (per, LANES))
        accs = [bias] * CONV_STRIDE
        for o in range(CONV_KERNEL + CONV_STRIDE - 1):
            rows = ubuf[cc, pl.ds(r0 + off0 + o, per, stride=CONV_STRIDE), :]
            for c in range(CONV_STRIDE):
                j = o - c
                if 0 <= j < CONV_KERNEL:
                    accs[c] = accs[c] + cw_ref[j:j + 1, lanes] * rows
        for c in range(CONV_STRIDE):
            ybuf[cc, pl.ds(r0 + c, per, stride=CONV_STRIDE), :] = accs[c]


def _ln_rows(r0, lg_ref, lb_ref, ybuf, out_ref, col0):
    inv_n = 1.0 / CONV_WIDTH
    ys = [ybuf[cc, pl.ds(r0, CONV_ROWS), :] for cc in range(N_SLABS)]
    tot = ys[0]
    for cc in range(1, N_SLABS):
        tot = tot + ys[cc]
    mu = jnp.sum(tot, axis=-1, keepdims=True) * inv_n
    ycs = [y - mu for y in ys]
    sq = ycs[0] * ycs[0]
    for cc in range(1, N_SLABS):
        sq = sq + ycs[cc] * ycs[cc]
    rstd = lax.rsqrt(jnp.sum(sq, axis=-1, keepdims=True) * inv_n + NORM_EPS)
    for cc in range(N_SLABS):
        lanes = slice(cc * LANES, (cc + 1) * LANES)
        yn = ycs[cc] * rstd * lg_ref[:, lanes] + lb_ref[:, lanes]
        out_lanes = slice(col0 + cc * LANES, col0 + (cc + 1) * LANES)
        out_ref[pl.ds(r0, CONV_ROWS), out_lanes] = (yn * jax.nn.sigmoid(yn)).astype(BF16)


def _attention_block(b, kvh, sink_ref, q_ref, kbuf, vbuf, mix, first, last):
    nblk = TM_MIX // BLOCK
    r = lax.broadcasted_iota(jnp.int32, (BLOCK, 3 * BLOCK), 0)
    c = lax.broadcasted_iota(jnp.int32, (BLOCK, 3 * BLOCK), 1)
    lo = jnp.where(first & (b == 0), BLOCK, 0)
    hi = jnp.where(last & (b == nblk - 1), 2 * BLOCK, 3 * BLOCK)
    valid = (c >= r) & (c <= r + 2 * WINDOW) & (c >= lo) & (c < hi)
    row0 = b * BLOCK
    kv_cols = pl.ds(kvh * HEAD_DIM, HEAD_DIM)
    kwin = kbuf[pl.ds(row0, 3 * BLOCK), kv_cols]
    vwin = vbuf[pl.ds(row0, 3 * BLOCK), kv_cols]
    head_cols = [pl.ds((kvh * Q_PER_KV + g) * HEAD_DIM, HEAD_DIM) for g in range(Q_PER_KV)]
    qs = jnp.concatenate([q_ref[pl.ds(row0, BLOCK), cols] for cols in head_cols], axis=0)
    s = lax.dot_general(qs, kwin, (((1,), (1,)), ((), ())),
                        preferred_element_type=F32) * (HEAD_DIM ** -0.5)
    for g in range(Q_PER_KV):
        sg = jnp.where(valid, s[g * BLOCK:(g + 1) * BLOCK, :], MASK_VALUE)
        sink = sink_ref[kvh * Q_PER_KV + g]
        m = jnp.maximum(jnp.max(sg, axis=-1, keepdims=True), sink)
        p = jnp.exp(sg - m)
        denom = jnp.sum(p, axis=-1, keepdims=True) + jnp.exp(sink - m)
        pv = jnp.dot(p.astype(BF16), vwin, preferred_element_type=F32)
        mix[pl.ds(row0, BLOCK), head_cols[g]] = (pv / denom).astype(BF16)


def _mixer_kernel(sink_ref, x_ref, q_ref, kp_ref, kc_ref, kn_ref, vp_ref, vc_ref, vn_ref,
                  cv_ref, wo_ref, o_ref, kbuf, vbuf, mix, *, tiles_per_seq):
    tm = TM_MIX
    i = pl.program_id(0)
    first = (i % tiles_per_seq) == 0
    last = (i % tiles_per_seq) == tiles_per_seq - 1

    mix[:, ATTN_WIDTH:] = cv_ref[...]
    kbuf[0:BLOCK, :] = kp_ref[...]
    kbuf[BLOCK:BLOCK + tm, :] = kc_ref[...]
    kbuf[BLOCK + tm:, :] = kn_ref[...]
    vbuf[0:BLOCK, :] = vp_ref[...]
    vbuf[BLOCK:BLOCK + tm, :] = vc_ref[...]
    vbuf[BLOCK + tm:, :] = vn_ref[...]

    for b in range(tm // BLOCK):
        for kvh in range(N_KV_HEADS):
            _attention_block(b, kvh, sink_ref, q_ref, kbuf, vbuf, mix, first, last)

    o_ref[...] = x_ref[...] + jnp.dot(mix[...], wo_ref[...], preferred_element_type=F32)


def _mixer(x, q, k, v, cv, sink, w_out, seq):
    t = x.shape[0]
    tm = TM_MIX
    tiles_per_seq = seq // tm
    bpt = tm // BLOCK
    nkb = t // BLOCK
    row = lambda i: (i, 0)
    const = lambda i: (0, 0)
    kprev = lambda i: (jnp.maximum(i * bpt - 1, 0), 0)
    knext = lambda i: (jnp.minimum((i + 1) * bpt, nkb - 1), 0)
    kern = functools.partial(_mixer_kernel, tiles_per_seq=tiles_per_seq)
    return pl.pallas_call(
        kern,
        grid=(t // tm,),
        in_specs=[
            pl.BlockSpec(memory_space=pltpu.SMEM),
            pl.BlockSpec((tm, D_MODEL), row),
            pl.BlockSpec((tm, ATTN_WIDTH), row),
            pl.BlockSpec((BLOCK, KV_WIDTH), kprev),
            pl.BlockSpec((tm, KV_WIDTH), row),
            pl.BlockSpec((BLOCK, KV_WIDTH), knext),
            pl.BlockSpec((BLOCK, KV_WIDTH), kprev),
            pl.BlockSpec((tm, KV_WIDTH), row),
            pl.BlockSpec((BLOCK, KV_WIDTH), knext),
            pl.BlockSpec((tm, CONV_WIDTH), row),
            pl.BlockSpec((D_MODEL, D_MODEL), const, pipeline_mode=pl.Buffered(1)),
        ],
        out_specs=pl.BlockSpec((tm, D_MODEL), row),
        out_shape=jax.ShapeDtypeStruct((t, D_MODEL), F32),
        scratch_shapes=[
            pltpu.VMEM((tm + 2 * BLOCK, KV_WIDTH), BF16),
            pltpu.VMEM((tm + 2 * BLOCK, KV_WIDTH), BF16),
            pltpu.VMEM((tm, D_MODEL), BF16),
        ],
        compiler_params=pltpu.CompilerParams(
            dimension_semantics=("parallel",), vmem_limit_bytes=VMEM_LIMIT_BYTES),
        name="mixer",
    )(sink, x, q, k, k, k, v, v, v, cv, w_out)


def _mlp_kernel(x_ref, g_ref, wu_ref, wd_ref, fg_ref, o_ref, h_ref, *, final_norm):
    j = pl.program_id(1)

    def mlp_block(first):
        for c in range(TF_MLP // TF_CHUNK):
            cols = slice(c * TF_CHUNK, (c + 1) * TF_CHUNK)
            a = jnp.dot(h_ref[...], wu_ref[:, cols], preferred_element_type=F32)
            a = jnp.square(jnp.maximum(a, 0.0)).astype(BF16)
            d = jnp.dot(a, wd_ref[cols, :], preferred_element_type=F32)
            if first and c == 0:
                o_ref[...] = x_ref[...] + d
            else:
                o_ref[...] += d

    @pl.when(j == 0)
    def _():
        x = x_ref[...]
        rstd = lax.rsqrt(jnp.mean(x * x, axis=-1, keepdims=True) + NORM_EPS)
        for k0 in range(0, D_MODEL, MXU_K):
            cols = slice(k0, k0 + MXU_K)
            h_ref[:, cols] = (x_ref[:, cols] * rstd * g_ref[:, cols]).astype(BF16)
        mlp_block(first=True)

    @pl.when(j > 0)
    def _():
        mlp_block(first=False)

    if final_norm:
        @pl.when(j == pl.num_programs(1) - 1)
        def _():
            o_ref[...] = _rms_norm_f32(o_ref[...], fg_ref[...])


def _mlp(x, g, w_up, w_down, fg, final_norm):
    t = x.shape[0]
    tm, tf = TM_MLP, TF_MLP
    kern = functools.partial(_mlp_kernel, final_norm=final_norm)
    return pl.pallas_call(
        kern,
        grid=(t // tm, D_FF // tf),
        in_specs=[
            pl.BlockSpec((tm, D_MODEL), lambda i, j: (i, 0)),
            pl.BlockSpec((1, D_MODEL), lambda i, j: (0, 0)),
            pl.BlockSpec((D_MODEL, tf), lambda i, j: (0, j)),
            pl.BlockSpec((tf, D_MODEL), lambda i, j: (j, 0)),
            pl.BlockSpec((1, D_MODEL), lambda i, j: (0, 0)),
        ],
        out_specs=pl.BlockSpec((tm, D_MODEL), lambda i, j: (i, 0)),
        out_shape=jax.ShapeDtypeStruct((t, D_MODEL), F32),
        scratch_shapes=[pltpu.VMEM((tm, D_MODEL), BF16)],
        compiler_params=pltpu.CompilerParams(
            dimension_semantics=("parallel", "arbitrary"), vmem_limit_bytes=VMEM_LIMIT_BYTES),
        name="mlp",
    )(x, g, w_up, w_down, fg)


def _rope_tables(seq):
    half = HEAD_DIM // 2
    inv_freq = jnp.exp(-math.log(ROPE_THETA) * jnp.arange(half, dtype=F32) / half)
    ang = jnp.arange(seq, dtype=F32)[:, None] * inv_freq[None, :]
    cos = jnp.cos(ang)
    sin = jnp.sin(ang)
    return jnp.concatenate([cos, cos], axis=-1), jnp.concatenate([-sin, sin], axis=-1)


def _trunk(x3, params, final_norm, f32_weights=None, bf16_weights=None):
    b, seq, _ = x3.shape
    x = x3.reshape(b * seq, D_MODEL)
    cos, sin = _rope_tables(seq)
    depth = params["norm_mix"].shape[0]
    fg = final_norm.reshape(1, D_MODEL)
    if f32_weights is not None:
        bf16_weights = {name: [None] * depth for name in ("w_in", "w_out", "w_up", "w_down")}
        bf16_weights["w_in"][0] = f32_weights["w_in"][0].astype(BF16)
    wb = bf16_weights
    for l in range(depth):
        cast = ()
        if f32_weights is not None:
            cast = [(f32_weights[name], l) for name in ("w_out", "w_up", "w_down")]
            if l + 1 < depth:
                cast.append((f32_weights["w_in"], l + 1))
        q, k, v, cv, *converted = _inproj(
            x, params["norm_mix"][l], wb["w_in"][l], cos, sin, params["conv_w"][l],
            params["conv_b"][l], params["conv_ln_g"][l], params["conv_ln_b"][l], seq, cast)
        if f32_weights is not None:
            wb["w_out"][l], wb["w_up"][l], wb["w_down"][l] = converted[:3]
            if l + 1 < depth:
                wb["w_in"][l + 1] = converted[3]
        x = _mixer(x, q, k, v, cv, params["attn_sink"][l], wb["w_out"][l], seq)
        x = _mlp(x, params["norm_ffn"][l], wb["w_up"][l], wb["w_down"][l], fg,
                 final_norm=(l == depth - 1))
    return x.reshape(b, seq, D_MODEL), wb


def kernel(x_prompt, x_sample, norm_mix, w_in, attn_sink, conv_w, conv_b, conv_ln_g, conv_ln_b,
           w_out, norm_ffn, w_up, w_down, final_norm):
    depth = w_in.shape[0]
    params = {
        "norm_mix": norm_mix.reshape(depth, 1, D_MODEL),
        "attn_sink": attn_sink,
        "conv_w": jnp.pad(conv_w, ((0, 0), (0, 1), (0, 0))),
        "conv_b": conv_b.reshape(depth, 1, CONV_WIDTH),
        "conv_ln_g": conv_ln_g.reshape(depth, 1, CONV_WIDTH),
        "conv_ln_b": conv_ln_b.reshape(depth, 1, CONV_WIDTH),
        "norm_ffn": norm_ffn.reshape(depth, 1, D_MODEL),
    }
    f32_weights = {"w_in": w_in, "w_out": w_out, "w_up": w_up, "w_down": w_down}
    y_prompt, bf16_weights = _trunk(x_prompt, params, final_norm, f32_weights=f32_weights)
    y_sample, _ = _trunk(x_sample, params, final_norm, bf16_weights=bf16_weights)
    return (y_prompt, y_sample)
```

```python
import functools
import math

import jax
import jax.numpy as jnp
from jax import lax
from jax.experimental import pallas as pl
from jax.experimental.pallas import tpu as pltpu

D_MODEL = 2048
HEAD_DIM = 128
N_Q_HEADS = 8
N_KV_HEADS = 2
Q_PER_KV = N_Q_HEADS // N_KV_HEADS
ATTN_WIDTH = N_Q_HEADS * HEAD_DIM
CONV_WIDTH = D_MODEL - ATTN_WIDTH
KV_WIDTH = N_KV_HEADS * HEAD_DIM
IN_WIDTH = ATTN_WIDTH + 2 * KV_WIDTH + 2 * CONV_WIDTH
WINDOW = 128
BLOCK = 128
CONV_KERNEL = 31
CONV_PAD = CONV_KERNEL // 2
D_FF = 4 * D_MODEL
ROPE_THETA = 10000.0
NORM_EPS = 1e-5
MASK_VALUE = -1e30
LOG2E = math.log2(math.e)

V7X_VMEM_BYTES = 64 * 1024 * 1024
VMEM_LIMIT_BYTES = V7X_VMEM_BYTES - 8 * 1024 * 1024

TM_IN = 512
TM_MIX = 512
TM_MLP = 1024
TF_MLP = 1024
TF_CHUNK = 512
HALO = 16
CONV_ROWS = 64
CONV_STRIDE = 4
LANES = 128
BF16_SUBLANES = 16
MXU_K = 256
N_SLABS = CONV_WIDTH // LANES

BF16 = jnp.bfloat16
F32 = jnp.float32


def _rms_norm_f32(x, g):
    return x * lax.rsqrt(jnp.mean(x * x, axis=-1, keepdims=True) + NORM_EPS) * g


def _inproj_kernel(x_ref, xp_ref, xn_ref, g_ref, w_ref, cos_ref, sin_ref,
                   cw_ref, cb_ref, lg_ref, lb_ref, *refs, n_cast, tiles_per_seq):
    tm = TM_IN
    cast_in = refs[:n_cast]
    q_ref, k_ref, v_ref, cv_ref = refs[n_cast:n_cast + 4]
    cast_out = refs[n_cast + 4:2 * n_cast + 4]
    h_scr, ubuf, ybuf = refs[2 * n_cast + 4:]

    for src, dst in zip(cast_in, cast_out):
        dst[...] = src[...].astype(BF16)

    i = pl.program_id(0)
    first = (i % tiles_per_seq) == 0
    last = (i % tiles_per_seq) == tiles_per_seq - 1

    g = g_ref[...]
    h_scr[0:HALO, :] = _rms_norm_f32(xp_ref[...], g).astype(BF16)
    h_scr[HALO:HALO + tm, :] = _rms_norm_f32(x_ref[...], g).astype(BF16)
    h_scr[HALO + tm:, :] = _rms_norm_f32(xn_ref[...], g).astype(BF16)

    o1 = ATTN_WIDTH
    o2 = o1 + KV_WIDTH
    o3 = o2 + KV_WIDTH
    o4 = o3 + CONV_WIDTH

    h_ext = h_scr[...]
    pa = jnp.dot(h_ext, w_ref[:, o3:o4], preferred_element_type=F32)
    pg = jnp.dot(h_ext, w_ref[:, o4:], preferred_element_type=F32)
    u = pa * jax.nn.sigmoid(pg)
    zero_halo = jnp.zeros((HALO, LANES), F32)
    for cc in range(N_SLABS):
        lanes = slice(cc * LANES, (cc + 1) * LANES)
        ubuf[cc, 0:HALO, :] = jnp.where(first, zero_halo, u[0:HALO, lanes])
        ubuf[cc, HALO:HALO + tm, :] = u[HALO:HALO + tm, lanes]
        ubuf[cc, HALO + tm:, :] = jnp.where(last, zero_halo, u[HALO + tm:, lanes])
    for r0 in range(0, tm, CONV_ROWS):
        _conv_rows(r0, cw_ref, cb_ref, ubuf, ybuf)
        _ln_rows(r0, lg_ref, lb_ref, ybuf, cv_ref, 0)

    cos = cos_ref[...]
    sin = sin_ref[...]

    def rope(t):
        return t * cos + pltpu.roll(t, HEAD_DIM // 2, 1) * sin

    h = h_scr[HALO:HALO + tm, :]
    pq = jnp.dot(h, w_ref[:, 0:o1], preferred_element_type=F32)
    for hd in range(N_Q_HEADS):
        sl = slice(hd * HEAD_DIM, (hd + 1) * HEAD_DIM)
        q_ref[:, sl] = rope(pq[:, sl]).astype(BF16)
    pk = jnp.dot(h, w_ref[:, o1:o2], preferred_element_type=F32)
    for hd in range(N_KV_HEADS):
        sl = slice(hd * HEAD_DIM, (hd + 1) * HEAD_DIM)
        k_ref[:, sl] = rope(pk[:, sl]).astype(BF16)
    v_ref[...] = jnp.dot(h, w_ref[:, o2:o3], preferred_element_type=F32).astype(BF16)


def _inproj(x, g, w_in, cos, sin, cw, cb, lg, lb, seq, cast=()):
    t = x.shape[0]
    tm = TM_IN
    n_steps = t // tm
    nseq_blk = seq // tm
    hpt = tm // HALO
    nhb = t // HALO
    row = lambda i: (i, 0)
    const = lambda i: (0, 0)
    pos = lambda i: (i % nseq_blk, 0)
    xprev = lambda i: (jnp.maximum(i * hpt - 1, 0), 0)
    xnext = lambda i: (jnp.minimum((i + 1) * hpt, nhb - 1), 0)
    cast_in_specs, cast_out_specs, cast_out_shapes = [], [], []
    for w, layer in cast:
        _, rows, cols = w.shape
        rps = rows // n_steps
        assert rps * n_steps == rows and rps % BF16_SUBLANES == 0
        cast_in_specs.append(pl.BlockSpec((None, rps, cols), lambda i, layer=layer: (layer, i, 0)))
        cast_out_specs.append(pl.BlockSpec((rps, cols), row))
        cast_out_shapes.append(jax.ShapeDtypeStruct((rows, cols), BF16))
    return pl.pallas_call(
        functools.partial(_inproj_kernel, n_cast=len(cast), tiles_per_seq=nseq_blk),
        grid=(n_steps,),
        in_specs=[
            pl.BlockSpec((tm, D_MODEL), row),
            pl.BlockSpec((HALO, D_MODEL), xprev),
            pl.BlockSpec((HALO, D_MODEL), xnext),
            pl.BlockSpec((1, D_MODEL), const),
            pl.BlockSpec((D_MODEL, IN_WIDTH), const, pipeline_mode=pl.Buffered(1)),
            pl.BlockSpec((tm, HEAD_DIM), pos),
            pl.BlockSpec((tm, HEAD_DIM), pos),
            pl.BlockSpec((CONV_KERNEL + 1, CONV_WIDTH), const),
            pl.BlockSpec((1, CONV_WIDTH), const),
            pl.BlockSpec((1, CONV_WIDTH), const),
            pl.BlockSpec((1, CONV_WIDTH), const),
        ] + cast_in_specs,
        out_specs=[
            pl.BlockSpec((tm, ATTN_WIDTH), row),
            pl.BlockSpec((tm, KV_WIDTH), row),
            pl.BlockSpec((tm, KV_WIDTH), row),
            pl.BlockSpec((tm, CONV_WIDTH), row),
        ] + cast_out_specs,
        out_shape=[
            jax.ShapeDtypeStruct((t, ATTN_WIDTH), BF16),
            jax.ShapeDtypeStruct((t, KV_WIDTH), BF16),
            jax.ShapeDtypeStruct((t, KV_WIDTH), BF16),
            jax.ShapeDtypeStruct((t, CONV_WIDTH), BF16),
        ] + cast_out_shapes,
        scratch_shapes=[
            pltpu.VMEM((tm + 2 * HALO, D_MODEL), BF16),
            pltpu.VMEM((N_SLABS, tm + 2 * HALO, LANES), F32),
            pltpu.VMEM((N_SLABS, tm, LANES), F32),
        ],
        compiler_params=pltpu.CompilerParams(
            dimension_semantics=("parallel",), vmem_limit_bytes=VMEM_LIMIT_BYTES),
        name="inproj",
    )(x, x, x, g, w_in, cos, sin, cw, cb, lg, lb, *[w for w, _ in cast])


def _conv_rows(r0, cw_ref, cb_ref, ubuf, ybuf):
    per = CONV_ROWS // CONV_STRIDE
    off0 = HALO - CONV_PAD
    for cc in range(N_SLABS):
        lanes = slice(cc * LANES, (cc + 1) * LANES)
        bias = jnp.broadcast_to(cb_ref[:, lanes], (per, LANES))
        accs = [bias] * CONV_STRIDE
        for o in range(CONV_KERNEL + CONV_STRIDE - 1):
            rows = ubuf[cc, pl.ds(r0 + off0 + o, per, stride=CONV_STRIDE), :]
            for c in range(CONV_STRIDE):
                j = o - c
                if 0 <= j < CONV_KERNEL:
                    accs[c] = accs[c] + cw_ref[j:j + 1, lanes] * rows
        for c in range(CONV_STRIDE):
            ybuf[cc, pl.ds(r0 + c, per, stride=CONV_STRIDE), :] = accs[c]


def _ln_rows(r0, lg_ref, lb_ref, ybuf, out_ref, col0):
    inv_n = 1.0 / CONV_WIDTH
    ys = [ybuf[cc, pl.ds(r0, CONV_ROWS), :] for cc in range(N_SLABS)]
    tot = ys[0]
    for cc in range(1, N_SLABS):
        tot = tot + ys[cc]
    mu = jnp.sum(tot, axis=-1, keepdims=True) * inv_n
    ycs = [y - mu for y in ys]
    sq = ycs[0] * ycs[0]
    for cc in range(1, N_SLABS):
        sq = sq + ycs[cc] * ycs[cc]
    rstd = lax.rsqrt(jnp.sum(sq, axis=-1, keepdims=True) * inv_n + NORM_EPS)
    for cc in range(N_SLABS):
        lanes = slice(cc * LANES, (cc + 1) * LANES)
        yn = ycs[cc] * rstd * lg_ref[:, lanes] + lb_ref[:, lanes]
        out_lanes = slice(col0 + cc * LANES, col0 + (cc + 1) * LANES)
        out_ref[pl.ds(r0, CONV_ROWS), out_lanes] = (yn * jax.nn.sigmoid(yn)).astype(BF16)


def _attention_block(b, kvh, sink_ref, q_ref, kbuf, vbuf, mix, first, last):
    nblk = TM_MIX // BLOCK
    r = lax.broadcasted_iota(jnp.int32, (BLOCK, BLOCK), 0)
    c = lax.broadcasted_iota(jnp.int32, (BLOCK, BLOCK), 1)
    prev_ok = (c >= r) & jnp.logical_not(first & (b == 0))
    next_ok = (c <= r) & jnp.logical_not(last & (b == nblk - 1))
    row0 = b * BLOCK
    kv_cols = pl.ds(kvh * HEAD_DIM, HEAD_DIM)
    kwin = kbuf[pl.ds(row0, 3 * BLOCK), kv_cols]
    vwin = vbuf[pl.ds(row0, 3 * BLOCK), kv_cols]
    head_cols = [pl.ds((kvh * Q_PER_KV + g) * HEAD_DIM, HEAD_DIM) for g in range(Q_PER_KV)]
    qs = jnp.concatenate([q_ref[pl.ds(row0, BLOCK), cols] for cols in head_cols], axis=0)
    s = lax.dot_general(qs, kwin, (((1,), (1,)), ((), ())),
                        preferred_element_type=F32)
    c1 = (HEAD_DIM ** -0.5) * LOG2E
    for g in range(Q_PER_KV):
        sr = s[g * BLOCK:(g + 1) * BLOCK, :]
        sg = jnp.concatenate([jnp.where(prev_ok, sr[:, 0:BLOCK], MASK_VALUE),
                              sr[:, BLOCK:2 * BLOCK],
                              jnp.where(next_ok, sr[:, 2 * BLOCK:], MASK_VALUE)], axis=1)
        sink2 = sink_ref[kvh * Q_PER_KV + g] * LOG2E
        m2 = jnp.maximum(jnp.max(sg, axis=-1, keepdims=True) * c1, sink2)
        p = jnp.exp2(sg * c1 - m2)
        denom = jnp.sum(p, axis=-1, keepdims=True) + jnp.exp2(sink2 - m2)
        pv = jnp.dot(p.astype(BF16), vwin, preferred_element_type=F32)
        mix[pl.ds(row0, BLOCK), head_cols[g]] = (pv / denom).astype(BF16)


def _mixer_kernel(sink_ref, x_ref, q_ref, kp_ref, kc_ref, kn_ref, vp_ref, vc_ref, vn_ref,
                  cv_ref, wo_ref, o_ref, kbuf, vbuf, mix, *, tiles_per_seq):
    tm = TM_MIX
    i = pl.program_id(0)
    first = (i % tiles_per_seq) == 0
    last = (i % tiles_per_seq) == tiles_per_seq - 1

    mix[:, ATTN_WIDTH:] = cv_ref[...]
    kbuf[0:BLOCK, :] = kp_ref[...]
    kbuf[BLOCK:BLOCK + tm, :] = kc_ref[...]
    kbuf[BLOCK + tm:, :] = kn_ref[...]
    vbuf[0:BLOCK, :] = vp_ref[...]
    vbuf[BLOCK:BLOCK + tm, :] = vc_ref[...]
    vbuf[BLOCK + tm:, :] = vn_ref[...]

    for b in range(tm // BLOCK):
        for kvh in range(N_KV_HEADS):
            _attention_block(b, kvh, sink_ref, q_ref, kbuf, vbuf, mix, first, last)

    o_ref[...] = x_ref[...] + jnp.dot(mix[...], wo_ref[...], preferred_element_type=F32)


def _mixer(x, q, k, v, cv, sink, w_out, seq):
    t = x.shape[0]
    tm = TM_MIX
    tiles_per_seq = seq // tm
    bpt = tm // BLOCK
    nkb = t // BLOCK
    row = lambda i: (i, 0)
    const = lambda i: (0, 0)
    kprev = lambda i: (jnp.maximum(i * bpt - 1, 0), 0)
    knext = lambda i: (jnp.minimum((i + 1) * bpt, nkb - 1), 0)
    kern = functools.partial(_mixer_kernel, tiles_per_seq=tiles_per_seq)
    return pl.pallas_call(
        kern,
        grid=(t // tm,),
        in_specs=[
            pl.BlockSpec(memory_space=pltpu.SMEM),
            pl.BlockSpec((tm, D_MODEL), row),
            pl.BlockSpec((tm, ATTN_WIDTH), row),
            pl.BlockSpec((BLOCK, KV_WIDTH), kprev),
            pl.BlockSpec((tm, KV_WIDTH), row),
            pl.BlockSpec((BLOCK, KV_WIDTH), knext),
            pl.BlockSpec((BLOCK, KV_WIDTH), kprev),
            pl.BlockSpec((tm, KV_WIDTH), row),
            pl.BlockSpec((BLOCK, KV_WIDTH), knext),
            pl.BlockSpec((tm, CONV_WIDTH), row),
            pl.BlockSpec((D_MODEL, D_MODEL), const, pipeline_mode=pl.Buffered(1)),
        ],
        out_specs=pl.BlockSpec((tm, D_MODEL), row),
        out_shape=jax.ShapeDtypeStruct((t, D_MODEL), F32),
        scratch_shapes=[
            pltpu.VMEM((tm + 2 * BLOCK, KV_WIDTH), BF16),
            pltpu.VMEM((tm + 2 * BLOCK, KV_WIDTH), BF16),
            pltpu.VMEM((tm, D_MODEL), BF16),
        ],
        compiler_params=pltpu.CompilerParams(
            dimension_semantics=("parallel",), vmem_limit_bytes=VMEM_LIMIT_BYTES),
        name="mixer",
    )(sink, x, q, k, k, k, v, v, v, cv, w_out)


def _mlp_kernel(x_ref, g_ref, wu_ref, wd_ref, fg_ref, o_ref, h_ref, *, final_norm):
    j = pl.program_id(1)

    def mlp_block(first):
        for c in range(TF_MLP // TF_CHUNK):
            cols = slice(c * TF_CHUNK, (c + 1) * TF_CHUNK)
            a = jnp.dot(h_ref[...], wu_ref[:, cols], preferred_element_type=F32)
            a = jnp.square(jnp.maximum(a, 0.0)).astype(BF16)
            d = jnp.dot(a, wd_ref[cols, :], preferred_element_type=F32)
            if first and c == 0:
                o_ref[...] = x_ref[...] + d
            else:
                o_ref[...] += d

    @pl.when(j == 0)
    def _():
        x = x_ref[...]
        rstd = lax.rsqrt(jnp.mean(x * x, axis=-1, keepdims=True) + NORM_EPS)
        for k0 in range(0, D_MODEL, MXU_K):
            cols = slice(k0, k0 + MXU_K)
            h_ref[:, cols] = (x_ref[:, cols] * rstd * g_ref[:, cols]).astype(BF16)
        mlp_block(first=True)

    @pl.when(j > 0)
    def _():
        mlp_block(first=False)

    if final_norm:
        @pl.when(j == pl.num_programs(1) - 1)
        def _():
            o_ref[...] = _rms_norm_f32(o_ref[...], fg_ref[...])


def _mlp(x, g, w_up, w_down, fg, final_norm):
    t = x.shape[0]
    tm, tf = TM_MLP, TF_MLP
    kern = functools.partial(_mlp_kernel, final_norm=final_norm)
    return pl.pallas_call(
        kern,
        grid=(t // tm, D_FF // tf),
        in_specs=[
            pl.BlockSpec((tm, D_MODEL), lambda i, j: (i, 0)),
            pl.BlockSpec((1, D_MODEL), lambda i, j: (0, 0)),
            pl.BlockSpec((D_MODEL, tf), lambda i, j: (0, j)),
            pl.BlockSpec((tf, D_MODEL), lambda i, j: (j, 0)),
            pl.BlockSpec((1, D_MODEL), lambda i, j: (0, 0)),
        ],
        out_specs=pl.BlockSpec((tm, D_MODEL), lambda i, j: (i, 0)),
        out_shape=jax.ShapeDtypeStruct((t, D_MODEL), F32),
        scratch_shapes=[pltpu.VMEM((tm, D_MODEL), BF16)],
        compiler_params=pltpu.CompilerParams(
            dimension_semantics=("parallel", "arbitrary"), vmem_limit_bytes=VMEM_LIMIT_BYTES),
        name="mlp",
    )(x, g, w_up, w_down, fg)


def _rope_tables(seq):
    half = HEAD_DIM // 2
    inv_freq = jnp.exp(-math.log(ROPE_THETA) * jnp.arange(half, dtype=F32) / half)
    ang = jnp.arange(seq, dtype=F32)[:, None] * inv_freq[None, :]
    cos = jnp.cos(ang)
    sin = jnp.sin(ang)
    return jnp.concatenate([cos, cos], axis=-1), jnp.concatenate([-sin, sin], axis=-1)


def _trunk(x3, params, final_norm, f32_weights=None, bf16_weights=None):
    b, seq, _ = x3.shape
    x = x3.reshape(b * seq, D_MODEL)
    cos, sin = _rope_tables(seq)
    depth = params["norm_mix"].shape[0]
    fg = final_norm.reshape(1, D_MODEL)
    if f32_weights is not None:
        bf16_weights = {name: [None] * depth for name in ("w_in", "w_out", "w_up", "w_down")}
        bf16_weights["w_in"][0] = f32_weights["w_in"][0].astype(BF16)
    wb = bf16_weights
    for l in range(depth):
        cast = ()
        if f32_weights is not None:
            cast = [(f32_weights[name], l) for name in ("w_out", "w_up", "w_down")]
            if l + 1 < depth:
                cast.append((f32_weights["w_in"], l + 1))
        q, k, v, cv, *converted = _inproj(
            x, params["norm_mix"][l], wb["w_in"][l], cos, sin, params["conv_w"][l],
            params["conv_b"][l], params["conv_ln_g"][l], params["conv_ln_b"][l], seq, cast)
        if f32_weights is not None:
            wb["w_out"][l], wb["w_up"][l], wb["w_down"][l] = converted[:3]
            if l + 1 < depth:
                wb["w_in"][l + 1] = converted[3]
        x = _mixer(x, q, k, v, cv, params["attn_sink"][l], wb["w_out"][l], seq)
        x = _mlp(x, params["norm_ffn"][l], wb["w_up"][l], wb["w_down"][l], fg,
                 final_norm=(l == depth - 1))
    return x.reshape(b, seq, D_MODEL), wb


def kernel(x_prompt, x_sample, norm_mix, w_in, attn_sink, conv_w, conv_b, conv_ln_g, conv_ln_b,
           w_out, norm_ffn, w_up, w_down, final_norm):
    depth = w_in.shape[0]
    params = {
        "norm_mix": norm_mix.reshape(depth, 1, D_MODEL),
        "attn_sink": attn_sink,
        "conv_w": jnp.pad(conv_w, ((0, 0), (0, 1), (0, 0))),
        "conv_b": conv_b.reshape(depth, 1, CONV_WIDTH),
        "conv_ln_g": conv_ln_g.reshape(depth, 1, CONV_WIDTH),
        "conv_ln_b": conv_ln_b.reshape(depth, 1, CONV_WIDTH),
        "norm_ffn": norm_ffn.reshape(depth, 1, D_MODEL),
    }
    f32_weights = {"w_in": w_in, "w_out": w_out, "w_up": w_up, "w_down": w_down}
    y_prompt, bf16_weights = _trunk(x_prompt, params, final_norm, f32_weights=f32_weights)
    y_sample, _ = _trunk(x_sample, params, final_norm, bf16_weights=bf16_weights)
    return (y_prompt, y_sample)
```

```python
import functools
import math

import jax
import jax.numpy as jnp
from jax import lax
from jax.experimental import pallas as pl
from jax.experimental.pallas import tpu as pltpu

D_MODEL = 2048
HEAD_DIM = 128
N_Q_HEADS = 8
N_KV_HEADS = 2
Q_PER_KV = N_Q_HEADS // N_KV_HEADS
ATTN_WIDTH = N_Q_HEADS * HEAD_DIM
CONV_WIDTH = D_MODEL - ATTN_WIDTH
KV_WIDTH = N_KV_HEADS * HEAD_DIM
IN_WIDTH = ATTN_WIDTH + 2 * KV_WIDTH + 2 * CONV_WIDTH
WINDOW = 128
BLOCK = 128
CONV_KERNEL = 31
CONV_PAD = CONV_KERNEL // 2
D_FF = 4 * D_MODEL
ROPE_THETA = 10000.0
NORM_EPS = 1e-5
MASK_VALUE = -1e30

V7X_VMEM_BYTES = 64 * 1024 * 1024
VMEM_LIMIT_BYTES = V7X_VMEM_BYTES - 8 * 1024 * 1024
MLP_VMEM_LIMIT_BYTES = V7X_VMEM_BYTES - 3 * 1024 * 1024

TM_IN = 512
TM_MIX = 512
TM_MLP = 1024
TF_MLP = 1024
TF_CHUNK = 512
HALO = 16
CONV_ROWS = 64
CONV_STRIDE = 4
LANES = 128
BF16_SUBLANES = 16
MXU_K = 256
N_SLABS = CONV_WIDTH // LANES

BF16 = jnp.bfloat16
F32 = jnp.float32


def _rms_norm_f32(x, g):
    return x * lax.rsqrt(jnp.mean(x * x, axis=-1, keepdims=True) + NORM_EPS) * g


def _inproj_kernel(x_ref, xp_ref, xn_ref, g_ref, w_ref, cos_ref, sin_ref,
                   cw_ref, cb_ref, lg_ref, lb_ref, *refs, n_cast, tiles_per_seq):
    tm = TM_IN
    cast_in = refs[:n_cast]
    q_ref, k_ref, v_ref, cv_ref = refs[n_cast:n_cast + 4]
    cast_out = refs[n_cast + 4:2 * n_cast + 4]
    h_scr, ubuf, ybuf = refs[2 * n_cast + 4:]

    for src, dst in zip(cast_in, cast_out):
        dst[...] = src[...].astype(BF16)

    i = pl.program_id(0)
    first = (i % tiles_per_seq) == 0
    last = (i % tiles_per_seq) == tiles_per_seq - 1

    for src, r0, rows in ((xp_ref, 0, HALO), (x_ref, HALO, tm), (xn_ref, HALO + tm, HALO)):
        xs = src[...]
        rstd = lax.rsqrt(jnp.mean(xs * xs, axis=-1, keepdims=True) + NORM_EPS)
        for k0 in range(0, D_MODEL, MXU_K):
            cols = slice(k0, k0 + MXU_K)
            h_scr[r0:r0 + rows, cols] = (src[:, cols] * rstd * g_ref[:, cols]).astype(BF16)

    o1 = ATTN_WIDTH
    o2 = o1 + KV_WIDTH
    o3 = o2 + KV_WIDTH
    o4 = o3 + CONV_WIDTH

    h_ext = h_scr[...]
    pa = jnp.dot(h_ext, w_ref[:, o3:o4], preferred_element_type=F32)
    pg = jnp.dot(h_ext, w_ref[:, o4:], preferred_element_type=F32)
    u = pa * jax.nn.sigmoid(pg)
    zero_halo = jnp.zeros((HALO, LANES), F32)
    for cc in range(N_SLABS):
        lanes = slice(cc * LANES, (cc + 1) * LANES)
        ubuf[cc, 0:HALO, :] = jnp.where(first, zero_halo, u[0:HALO, lanes])
        ubuf[cc, HALO:HALO + tm, :] = u[HALO:HALO + tm, lanes]
        ubuf[cc, HALO + tm:, :] = jnp.where(last, zero_halo, u[HALO + tm:, lanes])
    for r0 in range(0, tm, CONV_ROWS):
        _conv_rows(r0, cw_ref, cb_ref, ubuf, ybuf)
        _ln_rows(r0, lg_ref, lb_ref, ybuf, cv_ref, 0)

    cos = cos_ref[...]
    sin = sin_ref[...]

    def rope(t):
        return t * cos + pltpu.roll(t, HEAD_DIM // 2, 1) * sin

    h = h_scr[HALO:HALO + tm, :]
    pq = jnp.dot(h, w_ref[:, 0:o1], preferred_element_type=F32)
    for hd in range(N_Q_HEADS):
        sl = slice(hd * HEAD_DIM, (hd + 1) * HEAD_DIM)
        q_ref[:, sl] = rope(pq[:, sl]).astype(BF16)
    pk = jnp.dot(h, w_ref[:, o1:o2], preferred_element_type=F32)
    for hd in range(N_KV_HEADS):
        sl = slice(hd * HEAD_DIM, (hd + 1) * HEAD_DIM)
        k_ref[:, sl] = rope(pk[:, sl]).astype(BF16)
    v_ref[...] = jnp.dot(h, w_ref[:, o2:o3], preferred_element_type=F32).astype(BF16)


def _inproj(x, g, w_in, cos, sin, cw, cb, lg, lb, seq, cast=()):
    t = x.shape[0]
    tm = TM_IN
    n_steps = t // tm
    nseq_blk = seq // tm
    hpt = tm // HALO
    nhb = t // HALO
    row = lambda i: (i, 0)
    const = lambda i: (0, 0)
    pos = lambda i: (i % nseq_blk, 0)
    xprev = lambda i: (jnp.maximum(i * hpt - 1, 0), 0)
    xnext = lambda i: (jnp.minimum((i + 1) * hpt, nhb - 1), 0)
    cast_in_specs, cast_out_specs, cast_out_shapes = [], [], []
    for w, layer in cast:
        _, rows, cols = w.shape
        rps = rows // n_steps
        assert rps * n_steps == rows and rps % BF16_SUBLANES == 0
        cast_in_specs.append(pl.BlockSpec((None, rps, cols), lambda i, layer=layer: (layer, i, 0)))
        cast_out_specs.append(pl.BlockSpec((rps, cols), row))
        cast_out_shapes.append(jax.ShapeDtypeStruct((rows, cols), BF16))
    return pl.pallas_call(
        functools.partial(_inproj_kernel, n_cast=len(cast), tiles_per_seq=nseq_blk),
        grid=(n_steps,),
        in_specs=[
            pl.BlockSpec((tm, D_MODEL), row),
            pl.BlockSpec((HALO, D_MODEL), xprev),
            pl.BlockSpec((HALO, D_MODEL), xnext),
            pl.BlockSpec((1, D_MODEL), const),
            pl.BlockSpec((D_MODEL, IN_WIDTH), const, pipeline_mode=pl.Buffered(1)),
            pl.BlockSpec((tm, HEAD_DIM), pos),
            pl.BlockSpec((tm, HEAD_DIM), pos),
            pl.BlockSpec((CONV_KERNEL + 1, CONV_WIDTH), const),
            pl.BlockSpec((1, CONV_WIDTH), const),
            pl.BlockSpec((1, CONV_WIDTH), const),
            pl.BlockSpec((1, CONV_WIDTH), const),
        ] + cast_in_specs,
        out_specs=[
            pl.BlockSpec((tm, ATTN_WIDTH), row),
            pl.BlockSpec((tm, KV_WIDTH), row),
            pl.BlockSpec((tm, KV_WIDTH), row),
            pl.BlockSpec((tm, CONV_WIDTH), row),
        ] + cast_out_specs,
        out_shape=[
            jax.ShapeDtypeStruct((t, ATTN_WIDTH), BF16),
            jax.ShapeDtypeStruct((t, KV_WIDTH), BF16),
            jax.ShapeDtypeStruct((t, KV_WIDTH), BF16),
            jax.ShapeDtypeStruct((t, CONV_WIDTH), BF16),
        ] + cast_out_shapes,
        scratch_shapes=[
            pltpu.VMEM((tm + 2 * HALO, D_MODEL), BF16),
            pltpu.VMEM((N_SLABS, tm + 2 * HALO, LANES), F32),
            pltpu.VMEM((N_SLABS, tm, LANES), F32),
        ],
        compiler_params=pltpu.CompilerParams(
            dimension_semantics=("parallel",), vmem_limit_bytes=VMEM_LIMIT_BYTES),
        name="inproj",
    )(x, x, x, g, w_in, cos, sin, cw, cb, lg, lb, *[w for w, _ in cast])


def _conv_rows(r0, cw_ref, cb_ref, ubuf, ybuf):
    per = CONV_ROWS // CONV_STRIDE
    off0 = HALO - CONV_PAD
    for cc in range(N_SLABS):
        lanes = slice(cc * LANES, (cc + 1) * LANES)
        bias = jnp.broadcast_to(cb_ref[:, lanes], (per, LANES))
        accs = [bias] * CONV_STRIDE
        for o in range(CONV_KERNEL + CONV_STRIDE - 1):
            rows = ubuf[cc, pl.ds(r0 + off0 + o, per, stride=CONV_STRIDE), :]
            for c in range(CONV_STRIDE):
                j = o - c
                if 0 <= j < CONV_KERNEL:
                    accs[c] = accs[c] + cw_ref[j:j + 1, lanes] * rows
        for c in range(CONV_STRIDE):
            ybuf[cc, pl.ds(r0 + c, per, stride=CONV_STRIDE), :] = accs[c]


def _ln_rows(r0, lg_ref, lb_ref, ybuf, out_ref, col0):
    inv_n = 1.0 / CONV_WIDTH
    ys = [ybuf[cc, pl.ds(r0, CONV_ROWS), :] for cc in range(N_SLABS)]
    tot = ys[0]
    for cc in range(1, N_SLABS):
        tot = tot + ys[cc]
    mu = jnp.sum(tot, axis=-1, keepdims=True) * inv_n
    ycs = [y - mu for y in ys]
    sq = ycs[0] * ycs[0]
    for cc in range(1, N_SLABS):
        sq = sq + ycs[cc] * ycs[cc]
    rstd = lax.rsqrt(jnp.sum(sq, axis=-1, keepdims=True) * inv_n + NORM_EPS)
    for cc in range(N_SLABS):
        lanes = slice(cc * LANES, (cc + 1) * LANES)
        yn = ycs[cc] * rstd * lg_ref[:, lanes] + lb_ref[:, lanes]
        out_lanes = slice(col0 + cc * LANES, col0 + (cc + 1) * LANES)
        out_ref[pl.ds(r0, CONV_ROWS), out_lanes] = (yn * jax.nn.sigmoid(yn)).astype(BF16)


def _attention_block(b, kvh, sink_ref, q_ref, kbuf, vbuf, mix, first, last):
    nblk = TM_MIX // BLOCK
    r = lax.broadcasted_iota(jnp.int32, (BLOCK, 3 * BLOCK), 0)
    c = lax.broadcasted_iota(jnp.int32, (BLOCK, 3 * BLOCK), 1)
    lo = jnp.where(first & (b == 0), BLOCK, 0)
    hi = jnp.where(last & (b == nblk - 1), 2 * BLOCK, 3 * BLOCK)
    valid = (c >= r) & (c <= r + 2 * WINDOW) & (c >= lo) & (c < hi)
    row0 = b * BLOCK
    kv_cols = pl.ds(kvh * HEAD_DIM, HEAD_DIM)
    kwin = kbuf[pl.ds(row0, 3 * BLOCK), kv_cols]
    vwin = vbuf[pl.ds(row0, 3 * BLOCK), kv_cols]
    head_cols = [pl.ds((kvh * Q_PER_KV + g) * HEAD_DIM, HEAD_DIM) for g in range(Q_PER_KV)]
    qs = jnp.concatenate([q_ref[pl.ds(row0, BLOCK), cols] for cols in head_cols], axis=0)
    s = lax.dot_general(qs, kwin, (((1,), (1,)), ((), ())),
                        preferred_element_type=F32) * (HEAD_DIM ** -0.5)
    for g in range(Q_PER_KV):
        sg = jnp.where(valid, s[g * BLOCK:(g + 1) * BLOCK, :], MASK_VALUE)
        sink = sink_ref[kvh * Q_PER_KV + g]
        m = jnp.maximum(jnp.max(sg, axis=-1, keepdims=True), sink)
        p = jnp.exp(sg - m)
        denom = jnp.sum(p, axis=-1, keepdims=True) + jnp.exp(sink - m)
        pv = jnp.dot(p.astype(BF16), vwin, preferred_element_type=F32)
        mix[pl.ds(row0, BLOCK), head_cols[g]] = (pv / denom).astype(BF16)


def _mixer_kernel(sink_ref, x_ref, q_ref, kp_ref, kc_ref, kn_ref, vp_ref, vc_ref, vn_ref,
                  cv_ref, wo_ref, o_ref, kbuf, vbuf, mix, *, tiles_per_seq):
    tm = TM_MIX
    i = pl.program_id(0)
    first = (i % tiles_per_seq) == 0
    last = (i % tiles_per_seq) == tiles_per_seq - 1

    mix[:, ATTN_WIDTH:] = cv_ref[...]
    kbuf[0:BLOCK, :] = kp_ref[...]
    kbuf[BLOCK:BLOCK + tm, :] = kc_ref[...]
    kbuf[BLOCK + tm:, :] = kn_ref[...]
    vbuf[0:BLOCK, :] = vp_ref[...]
    vbuf[BLOCK:BLOCK + tm, :] = vc_ref[...]
    vbuf[BLOCK + tm:, :] = vn_ref[...]

    for b in range(tm // BLOCK):
        for kvh in range(N_KV_HEADS):
            _attention_block(b, kvh, sink_ref, q_ref, kbuf, vbuf, mix, first, last)

    o_ref[...] = x_ref[...] + jnp.dot(mix[...], wo_ref[...], preferred_element_type=F32)


def _mixer(x, q, k, v, cv, sink, w_out, seq):
    t = x.shape[0]
    tm = TM_MIX
    tiles_per_seq = seq // tm
    bpt = tm // BLOCK
    nkb = t // BLOCK
    row = lambda i: (i, 0)
    const = lambda i: (0, 0)
    kprev = lambda i: (jnp.maximum(i * bpt - 1, 0), 0)
    knext = lambda i: (jnp.minimum((i + 1) * bpt, nkb - 1), 0)
    kern = functools.partial(_mixer_kernel, tiles_per_seq=tiles_per_seq)
    return pl.pallas_call(
        kern,
        grid=(t // tm,),
        in_specs=[
            pl.BlockSpec(memory_space=pltpu.SMEM),
            pl.BlockSpec((tm, D_MODEL), row),
            pl.BlockSpec((tm, ATTN_WIDTH), row),
            pl.BlockSpec((BLOCK, KV_WIDTH), kprev),
            pl.BlockSpec((tm, KV_WIDTH), row),
            pl.BlockSpec((BLOCK, KV_WIDTH), knext),
            pl.BlockSpec((BLOCK, KV_WIDTH), kprev),
            pl.BlockSpec((tm, KV_WIDTH), row),
            pl.BlockSpec((BLOCK, KV_WIDTH), knext),
            pl.BlockSpec((tm, CONV_WIDTH), row),
            pl.BlockSpec((D_MODEL, D_MODEL), const, pipeline_mode=pl.Buffered(1)),
        ],
        out_specs=pl.BlockSpec((tm, D_MODEL), row),
        out_shape=jax.ShapeDtypeStruct((t, D_MODEL), F32),
        scratch_shapes=[
            pltpu.VMEM((tm + 2 * BLOCK, KV_WIDTH), BF16),
            pltpu.VMEM((tm + 2 * BLOCK, KV_WIDTH), BF16),
            pltpu.VMEM((tm, D_MODEL), BF16),
        ],
        compiler_params=pltpu.CompilerParams(
            dimension_semantics=("parallel",), vmem_limit_bytes=VMEM_LIMIT_BYTES),
        name="mixer",
    )(sink, x, q, k, k, k, v, v, v, cv, w_out)


def _mlp_kernel(x_ref, g_ref, wu_hbm, wd_hbm, fg_ref, o_ref, h_ref, wu_buf, wd_buf, sem,
                *, final_norm, n_tiles):
    i = pl.program_id(0)
    n_blocks = D_FF // TF_MLP

    def copies(blk, slot):
        cols = pl.ds(blk * TF_MLP, TF_MLP)
        return (pltpu.make_async_copy(wu_hbm.at[:, cols], wu_buf.at[slot], sem.at[0, slot]),
                pltpu.make_async_copy(wd_hbm.at[cols, :], wd_buf.at[slot], sem.at[1, slot]))

    def start(blk, slot):
        for cp in copies(blk, slot):
            cp.start()

    def wait(blk, slot):
        for cp in copies(blk, slot):
            cp.wait()

    def mlp_block(slot, first):
        wu, wd = wu_buf.at[slot], wd_buf.at[slot]
        for c in range(TF_MLP // TF_CHUNK):
            cols = slice(c * TF_CHUNK, (c + 1) * TF_CHUNK)
            a = jnp.dot(h_ref[...], wu[:, cols], preferred_element_type=F32)
            a = jnp.square(jnp.maximum(a, 0.0)).astype(BF16)
            d = jnp.dot(a, wd[cols, :], preferred_element_type=F32)
            if first and c == 0:
                o_ref[...] = x_ref[...] + d
            else:
                o_ref[...] += d

    @pl.when(i == 0)
    def _():
        start(0, 0)

    wait(0, 0)
    start(1, 1)
    x = x_ref[...]
    rstd = lax.rsqrt(jnp.mean(x * x, axis=-1, keepdims=True) + NORM_EPS)
    for k0 in range(0, D_MODEL, MXU_K):
        cols = slice(k0, k0 + MXU_K)
        h_ref[:, cols] = (x_ref[:, cols] * rstd * g_ref[:, cols]).astype(BF16)
    mlp_block(0, first=True)

    def block_step(blk, carry):
        slot = blk % 2
        wait(blk, slot)

        @pl.when(jnp.logical_not((blk == n_blocks - 1) & (i == n_tiles - 1)))
        def _():
            start(lax.rem(blk + 1, n_blocks), 1 - slot)

        mlp_block(slot, first=False)
        return carry

    lax.fori_loop(1, n_blocks, block_step, 0)

    if final_norm:
        o_ref[...] = _rms_norm_f32(o_ref[...], fg_ref[...])


def _mlp(x, g, w_up, w_down, fg, final_norm):
    t = x.shape[0]
    tm, tf = TM_MLP, TF_MLP
    n_tiles = t // tm
    assert (D_FF // tf) % 2 == 0
    kern = functools.partial(_mlp_kernel, final_norm=final_norm, n_tiles=n_tiles)
    return pl.pallas_call(
        kern,
        grid=(n_tiles,),
        in_specs=[
            pl.BlockSpec((tm, D_MODEL), lambda i: (i, 0)),
            pl.BlockSpec((1, D_MODEL), lambda i: (0, 0)),
            pl.BlockSpec(memory_space=pl.ANY),
            pl.BlockSpec(memory_space=pl.ANY),
            pl.BlockSpec((1, D_MODEL), lambda i: (0, 0)),
        ],
        out_specs=pl.BlockSpec((tm, D_MODEL), lambda i: (i, 0)),
        out_shape=jax.ShapeDtypeStruct((t, D_MODEL), F32),
        scratch_shapes=[
            pltpu.VMEM((tm, D_MODEL), BF16),
            pltpu.VMEM((2, D_MODEL, tf), BF16),
            pltpu.VMEM((2, tf, D_MODEL), BF16),
            pltpu.SemaphoreType.DMA((2, 2)),
        ],
        compiler_params=pltpu.CompilerParams(
            dimension_semantics=("arbitrary",), vmem_limit_bytes=MLP_VMEM_LIMIT_BYTES),
        name="mlp",
    )(x, g, w_up, w_down, fg)


def _rope_tables(seq):
    half = HEAD_DIM // 2
    inv_freq = jnp.exp(-math.log(ROPE_THETA) * jnp.arange(half, dtype=F32) / half)
    ang = jnp.arange(seq, dtype=F32)[:, None] * inv_freq[None, :]
    cos = jnp.cos(ang)
    sin = jnp.sin(ang)
    return jnp.concatenate([cos, cos], axis=-1), jnp.concatenate([-sin, sin], axis=-1)


def _trunk(x3, params, final_norm, f32_weights=None, bf16_weights=None):
    b, seq, _ = x3.shape
    x = x3.reshape(b * seq, D_MODEL)
    cos, sin = _rope_tables(seq)
    depth = params["norm_mix"].shape[0]
    fg = final_norm.reshape(1, D_MODEL)
    if f32_weights is not None:
        bf16_weights = {name: [None] * depth for name in ("w_in", "w_out", "w_up", "w_down")}
        bf16_weights["w_in"][0] = f32_weights["w_in"][0].astype(BF16)
    wb = bf16_weights
    for l in range(depth):
        cast = ()
        if f32_weights is not None:
            cast = [(f32_weights[name], l) for name in ("w_out", "w_up", "w_down")]
            if l + 1 < depth:
                cast.append((f32_weights["w_in"], l + 1))
        q, k, v, cv, *converted = _inproj(
            x, params["norm_mix"][l], wb["w_in"][l], cos, sin, params["conv_w"][l],
            params["conv_b"][l], params["conv_ln_g"][l], params["conv_ln_b"][l], seq, cast)
        if f32_weights is not None:
            wb["w_out"][l], wb["w_up"][l], wb["w_down"][l] = converted[:3]
            if l + 1 < depth:
                wb["w_in"][l + 1] = converted[3]
        x = _mixer(x, q, k, v, cv, params["attn_sink"][l], wb["w_out"][l], seq)
        x = _mlp(x, params["norm_ffn"][l], wb["w_up"][l], wb["w_down"][l], fg,
                 final_norm=(l == depth - 1))
    return x.reshape(b, seq, D_MODEL), wb


def kernel(x_prompt, x_sample, norm_mix, w_in, attn_sink, conv_w, conv_b, conv_ln_g, conv_ln_b,
           w_out, norm_ffn, w_up, w_down, final_norm):
    depth = w_in.shape[0]
    params = {
        "norm_mix": norm_mix.reshape(depth, 1, D_MODEL),
        "attn_sink": attn_sink,
        "conv_w": jnp.pad(conv_w, ((0, 0), (0, 1), (0, 0))),
        "conv_b": conv_b.reshape(depth, 1, CONV_WIDTH),
        "conv_ln_g": conv_ln_g.reshape(depth, 1, CONV_WIDTH),
        "conv_ln_b": conv_ln_b.reshape(depth, 1, CONV_WIDTH),
        "norm_ffn": norm_ffn.reshape(depth, 1, D_MODEL),
    }
    f32_weights = {"w_in": w_in, "w_out": w_out, "w_up": w_up, "w_down": w_down}
    y_prompt, bf16_weights = _trunk(x_prompt, params, final_norm, f32_weights=f32_weights)
    y_sample, _ = _trunk(x_sample, params, final_norm, bf16_weights=bf16_weights)
    return (y_prompt, y_sample)
```
